```python
import jax, jax.numpy as jnp
from jax import lax
import numpy as np

D_MODEL = 1024
BATCH = 1
SEQ = 16384
DEPTH = 4

N_MIXERS = 3
N_A = len(range(0, DEPTH, N_MIXERS))
N_B = len(range(1, DEPTH, N_MIXERS))
N_C = len(range(2, DEPTH, N_MIXERS))
N_SUBLAYERS = 3
N_MOD = 3
CONV_WIDTH = 31
N_HEADS = 16
HEAD_DIM = D_MODEL // N_HEADS
Q_BLOCK = 128
POOL_WINDOWS = (2, 4, 8, 16)
POOL_GROUPS = len(POOL_WINDOWS)
POOL_GROUP_DIM = D_MODEL // POOL_GROUPS
D_FF = ((8 * D_MODEL // 3 + 127) // 128) * 128
EPS = 1e-6

kernel_name = "hybrid_conv_stickbreak_pool_macaron"


def _bc(v, ndim):
    return v.reshape((1,) * (ndim - v.ndim) + v.shape)


def rms_norm(x, g):
    xf = x.astype(jnp.float32)
    y = xf * lax.rsqrt(jnp.mean(xf * xf, axis=-1, keepdims=True) + EPS)
    return (y * _bc(g.astype(jnp.float32), x.ndim)).astype(x.dtype)


def layer_norm(x, g, b):
    xf = x.astype(jnp.float32)
    mu = jnp.mean(xf, axis=-1, keepdims=True)
    var = jnp.mean(jnp.square(xf - mu), axis=-1, keepdims=True)
    y = (xf - mu) * lax.rsqrt(var + EPS)
    y = y * _bc(g.astype(jnp.float32), x.ndim) + _bc(b.astype(jnp.float32), x.ndim)
    return y.astype(x.dtype)


def modulate(x, g, shift, scale):
    return rms_norm(x, g) * (1 + scale[:, None, :]) + shift[:, None, :]


def swiglu(h, w1, w3, w2):
    return (jax.nn.silu(h @ w1) * (h @ w3)) @ w2


def conformer_conv(h, w_in, b_in, dw, dw_b, ln_g, ln_b, w_out, b_out):
    u = jax.nn.glu(h @ w_in + _bc(b_in, 3), axis=-1)
    v = lax.conv_general_dilated(
        u, dw[:, None, :].astype(u.dtype), window_strides=(1,),
        padding=[(CONV_WIDTH - 1, 0)],
        dimension_numbers=("NWC", "WIO", "NWC"),
        feature_group_count=D_MODEL) + _bc(dw_b, 3)
    v = jax.nn.silu(layer_norm(v, ln_g, ln_b))
    return v @ w_out + _bc(b_out, 3)


def stick_breaking_attention(h, w_qkv, q_g, k_g, w_o):
    B, S, _ = h.shape
    qkv = h @ w_qkv
    q, k, v = jnp.split(qkv, 3, axis=-1)
    to_heads = lambda t: t.reshape(B, S, N_HEADS, HEAD_DIM).transpose(0, 2, 1, 3)
    q = rms_norm(to_heads(q), q_g)
    k = rms_norm(to_heads(k), k_g)
    qf = q.astype(jnp.float32) * (HEAD_DIM ** -0.5)
    kf = k.astype(jnp.float32)
    vf = to_heads(v).astype(jnp.float32)
    nb = S // Q_BLOCK
    q_blocks = qf.reshape(B, N_HEADS, nb, Q_BLOCK, HEAD_DIM).transpose(2, 0, 1, 3, 4)
    key_pos = jnp.arange(S)

    def block(args):
        qb, bi = args
        z = jnp.einsum('bhqd,bhkd->bhqk', qb, kf)
        t = bi * Q_BLOCK + jnp.arange(Q_BLOCK)
        mask = (key_pos[None, :] < t[:, None])[None, None]
        neg_log_keep = jnp.where(mask, jax.nn.softplus(z), 0.0)
        rcs = lax.cumsum(neg_log_keep, axis=3, reverse=True)
        suffix = jnp.concatenate([rcs[..., 1:], jnp.zeros_like(rcs[..., :1])], axis=-1)
        a = jnp.where(mask, jnp.exp(jax.nn.log_sigmoid(z) - suffix), 0.0)
        return jnp.einsum('bhqk,bhkd->bhqd', a, vf)

    o = lax.map(block, (q_blocks, jnp.arange(nb)))
    o = o.transpose(1, 0, 3, 2, 4).reshape(B, S, D_MODEL).astype(h.dtype)
    return o @ w_o


def multiscale_pool(h, p_w, p_b, p_scale):
    B, S, _ = h.shape
    hf = h.astype(jnp.float32)
    cs0 = jnp.concatenate([jnp.zeros((B, 1, D_MODEL), jnp.float32), lax.cumsum(hf, axis=1)], axis=1)
    t = jnp.arange(S)
    diffs = []
    for g, w in enumerate(POOL_WINDOWS):
        sl = slice(g * POOL_GROUP_DIM, (g + 1) * POOL_GROUP_DIM)
        c_g = cs0[:, :, sl]
        lag = jnp.concatenate([jnp.zeros((B, w - 1, POOL_GROUP_DIM), jnp.float32), c_g[:, :S + 1 - w]], axis=1)
        cnt = jnp.minimum(t + 1, w).astype(jnp.float32)[None, :, None]
        diffs.append((c_g[:, 1:] - lag) / cnt - hf[:, :, sl])
    d = jnp.stack(diffs, axis=2)
    y = jnp.einsum('bsgc,gcd->bsgd', d, p_w.astype(jnp.float32)) + _bc(p_b.astype(jnp.float32), 4)
    return (y.reshape(B, S, D_MODEL) * _bc(p_scale.astype(jnp.float32), 3)).astype(h.dtype)


def setup_inputs(seed: int = 0) -> dict:
    key = jax.random.key(seed)
    ks = jax.random.split(key, 26)
    n = lambda k, shape, s: jax.random.normal(k, shape, jnp.float32) * s
    D, F, G = D_MODEL, D_FF, POOL_GROUP_DIM
    return {
        "x": n(ks[0], (BATCH, SEQ, D), 1.0),
        "c": n(ks[1], (BATCH, D), 1.0),
        "cond_w": n(ks[2], (D, D), D ** -0.5),
        "cond_b": n(ks[3], (D,), 0.02),
        "ada_w": n(ks[4], (DEPTH, D, N_SUBLAYERS * N_MOD * D), 0.1 * D ** -0.5),
        "ada_b": n(ks[5], (DEPTH, N_SUBLAYERS * N_MOD * D), 0.02),
        "norm_g": 1.0 + n(ks[6], (DEPTH, N_SUBLAYERS, D), 0.02),
        "ffn_w1": n(ks[7], (DEPTH, 2, D, F), D ** -0.5),
        "ffn_w3": n(ks[8], (DEPTH, 2, D, F), D ** -0.5),
        "ffn_w2": n(ks[9], (DEPTH, 2, F, D), F ** -0.5),
        "a_w_in": n(ks[10], (N_A, D, 2 * D), D ** -0.5),
        "a_b_in": n(ks[11], (N_A, 2 * D), 0.02),
        "a_dw": n(ks[12], (N_A, CONV_WIDTH, D), CONV_WIDTH ** -0.5),
        "a_dw_b": n(ks[13], (N_A, D), 0.02),
        "a_ln_g": 1.0 + n(ks[14], (N_A, D), 0.02),
        "a_ln_b": n(ks[15], (N_A, D), 0.02),
        "a_w_out": n(ks[16], (N_A, D, D), D ** -0.5),
        "a_b_out": n(ks[17], (N_A, D), 0.02),
        "b_w_qkv": n(ks[18], (N_B, D, 3 * D), D ** -0.5),
        "b_q_g": 1.0 + n(ks[19], (N_B, HEAD_DIM), 0.02),
        "b_k_g": 1.0 + n(ks[20], (N_B, HEAD_DIM), 0.02),
        "b_w_o": n(ks[21], (N_B, D, D), D ** -0.5),
        "p_w": n(ks[22], (N_C, POOL_GROUPS, G, G), G ** -0.5),
        "p_b": n(ks[23], (N_C, POOL_GROUPS, G), 0.02),
        "p_scale": 1.0 + n(ks[24], (N_C, D), 0.1),
    }


def reference(x, c, cond_w, cond_b, ada_w, ada_b, norm_g, ffn_w1, ffn_w3, ffn_w2,
              a_w_in, a_b_in, a_dw, a_dw_b, a_ln_g, a_ln_b, a_w_out, a_b_out,
              b_w_qkv, b_q_g, b_k_g, b_w_o, p_w, p_b, p_scale):
    B = x.shape[0]
    e = jax.nn.silu(c @ cond_w + cond_b[None, :])
    ia = ib = ic = 0
    for i in range(DEPTH):
        mod = (e @ ada_w[i] + ada_b[i][None, :]).reshape(B, N_SUBLAYERS, N_MOD, D_MODEL)
        shift, scale, gate = mod[:, :, 0], mod[:, :, 1], 1 + mod[:, :, 2]
        h = modulate(x, norm_g[i, 0], shift[:, 0], scale[:, 0])
        x = x + 0.5 * gate[:, 0][:, None, :] * swiglu(h, ffn_w1[i, 0], ffn_w3[i, 0], ffn_w2[i, 0])
        h = modulate(x, norm_g[i, 1], shift[:, 1], scale[:, 1])
        kind = i % N_MIXERS
        if kind == 0:
            y = conformer_conv(h, a_w_in[ia], a_b_in[ia], a_dw[ia], a_dw_b[ia],
                               a_ln_g[ia], a_ln_b[ia], a_w_out[ia], a_b_out[ia])
            ia += 1
        elif kind == 1:
            y = stick_breaking_attention(h, b_w_qkv[ib], b_q_g[ib], b_k_g[ib], b_w_o[ib])
            ib += 1
        else:
            y = multiscale_pool(h, p_w[ic], p_b[ic], p_scale[ic])
            ic += 1
        x = x + gate[:, 1][:, None, :] * y
        h = modulate(x, norm_g[i, 2], shift[:, 2], scale[:, 2])
        x = x + 0.5 * gate[:, 2][:, None, :] * swiglu(h, ffn_w1[i, 1], ffn_w3[i, 1], ffn_w2[i, 1])
    return x
```

```python
import functools

import jax
import jax.numpy as jnp
from jax import lax
from jax.experimental import pallas as pl
from jax.experimental.pallas import tpu as pltpu

F32 = jnp.float32
BF16 = jnp.bfloat16

EPS = 1e-6
N_HEADS = 16
HEAD_DIM = 64
CONV_WIDTH = 31
POOL_WINDOWS = (2, 4, 8, 16)
LANES = 128
HEADS_PER_BLOCK = LANES // HEAD_DIM

ROW_TILE = 512
FF_CHUNK = 256
CONV_HALO = 32
CONV_ROWS = 64
CONV_LANES = 256
POOL_HALO = 16
Q_TILE = 256
K_TILE = 128
VMEM_LIMIT = 56 * 1024 * 1024


def _sigmoid(x):
    return 1.0 / (1.0 + jnp.exp(-x))


def _split_bf16(x):
    hi = x.astype(BF16)
    lo = (x - hi.astype(F32)).astype(BF16)
    return hi, lo


def _modulate(x, g, mod_ref):
    ms = jnp.mean(x * x, axis=-1, keepdims=True)
    gs = g * (1.0 + mod_ref[1:2, :])
    return (x * lax.rsqrt(ms + EPS)) * gs + mod_ref[0:1, :]


def _resident(shape):
    zeros = (0,) * len(shape)
    return pl.BlockSpec(shape, lambda *_: zeros, pipeline_mode=pl.Buffered(1))


def _row_spec(tile, width):
    return pl.BlockSpec((tile, width), lambda i: (i, 0))


def _params(semantics):
    return pltpu.CompilerParams(dimension_semantics=semantics, vmem_limit_bytes=VMEM_LIMIT)


def _cond_embed_kernel(c_ref, w_ref, b_ref, e_ref):
    t = jnp.sum(c_ref[...] * w_ref[...], axis=0, keepdims=True) + b_ref[...]
    e_ref[...] = t * _sigmoid(t)


def _cond_mod_kernel(e_ref, w_ref, b_ref, o_ref):
    o_ref[0] = jnp.sum(e_ref[...] * w_ref[0], axis=0, keepdims=True) + b_ref[0]


def _conditioning(c, cond_w, cond_b, ada_w, ada_b):
    d = cond_w.shape[0]
    depth, _, n = ada_w.shape
    e = pl.pallas_call(
        _cond_embed_kernel,
        out_shape=jax.ShapeDtypeStruct((1, d), F32),
        name="cond_embed",
    )(c.reshape(d, 1), cond_w, cond_b.reshape(1, d))
    tn = n // 8
    mod = pl.pallas_call(
        _cond_mod_kernel,
        grid=(depth, n // tn),
        in_specs=[
            pl.BlockSpec((d, 1), lambda l, j: (0, 0)),
            pl.BlockSpec((1, d, tn), lambda l, j: (l, 0, j)),
            pl.BlockSpec((1, 1, tn), lambda l, j: (l, 0, j)),
        ],
        out_specs=pl.BlockSpec((1, 1, tn), lambda l, j: (l, 0, j)),
        out_shape=jax.ShapeDtypeStruct((depth, 1, n), F32),
        compiler_params=_params(("arbitrary", "arbitrary")),
        name="cond_mod",
    )(e.reshape(d, 1), ada_w, ada_b.reshape(depth, 1, n))
    return mod.reshape(depth, 3, 3, d)


def _ffn_kernel(x_ref, g_ref, mod_ref, w1_ref, w3_ref, w2_ref, o_ref, h_ref, acc_ref):
    x = x_ref[...]
    h_ref[...] = _modulate(x, g_ref[...], mod_ref).astype(BF16)
    acc_ref[...] = jnp.zeros_like(acc_ref)

    def chunk(f, carry):
        h = h_ref[...]
        a = jnp.dot(h, w1_ref[f], preferred_element_type=F32)
        b = jnp.dot(h, w3_ref[f], preferred_element_type=F32)
        act = (a * _sigmoid(a) * b).astype(BF16)
        acc_ref[...] += jnp.dot(act, w2_ref[f], preferred_element_type=F32)
        return carry

    lax.fori_loop(0, w1_ref.shape[0], chunk, 0)
    o_ref[...] = x + (0.5 * (1.0 + mod_ref[2:3, :])) * acc_ref[...]


def _ffn(x, g, mod, w1, w3, w2):
    s, d = x.shape
    f = w1.shape[1]
    nf = f // FF_CHUNK
    w1c = w1.astype(BF16).reshape(d, nf, FF_CHUNK).transpose(1, 0, 2)
    w3c = w3.astype(BF16).reshape(d, nf, FF_CHUNK).transpose(1, 0, 2)
    w2c = w2.astype(BF16).reshape(nf, FF_CHUNK, d)
    return pl.pallas_call(
        _ffn_kernel,
        grid=(s // ROW_TILE,),
        in_specs=[
            _row_spec(ROW_TILE, d),
            _resident((1, d)),
            _resident((3, d)),
            _resident((nf, d, FF_CHUNK)),
            _resident((nf, d, FF_CHUNK)),
            _resident((nf, FF_CHUNK, d)),
        ],
        out_specs=_row_spec(ROW_TILE, d),
        out_shape=jax.ShapeDtypeStruct((s, d), F32),
        scratch_shapes=[pltpu.VMEM((ROW_TILE, d), BF16), pltpu.VMEM((ROW_TILE, d), F32)],
        compiler_params=_params(("arbitrary",)),
        name="ffn",
    )(x, g.reshape(1, d), mod, w1c, w3c, w2c)


def _conv_kernel(x_ref, g_ref, mod_ref, win_ref, bin_ref, dw_ref, dwb_ref, lng_ref, lnb_ref,
                 wout_ref, bout_ref, o_ref, u_ref, v_ref):
    tm, d = x_ref.shape
    i = pl.program_id(0)

    @pl.when(i == 0)
    def _():
        u_ref[0:CONV_HALO, :] = jnp.zeros((CONV_HALO, d), F32)

    @pl.when(i > 0)
    def _():
        u_ref[0:CONV_HALO, :] = u_ref[tm:tm + CONV_HALO, :]

    x = x_ref[...]
    h = _modulate(x, g_ref[...], mod_ref).astype(BF16)
    uv = jnp.dot(h, win_ref[...], preferred_element_type=F32) + bin_ref[...]
    u_ref[CONV_HALO:CONV_HALO + tm, :] = uv[:, :d] * _sigmoid(uv[:, d:])

    base = CONV_HALO - (CONV_WIDTH - 1)
    for r in range(tm // CONV_ROWS):
        r0 = r * CONV_ROWS
        for c in range(d // CONV_LANES):
            cs = slice(c * CONV_LANES, (c + 1) * CONV_LANES)
            acc = jnp.broadcast_to(dwb_ref[:, cs], (CONV_ROWS, CONV_LANES))
            for k in range(CONV_WIDTH):
                acc = acc + dw_ref[k:k + 1, cs] * u_ref[r0 + base + k:r0 + base + k + CONV_ROWS, cs]
            v_ref[r0:r0 + CONV_ROWS, cs] = acc

    v = v_ref[...]
    mu = jnp.mean(v, axis=-1, keepdims=True)
    vc = v - mu
    var = jnp.mean(vc * vc, axis=-1, keepdims=True)
    y = vc * lax.rsqrt(var + EPS) * lng_ref[...] + lnb_ref[...]
    y = (y * _sigmoid(y)).astype(BF16)
    y = jnp.dot(y, wout_ref[...], preferred_element_type=F32) + bout_ref[...]
    o_ref[...] = x + (1.0 + mod_ref[2:3, :]) * y


def _conv_mixer(x, g, mod, w_in, b_in, dw, dw_b, ln_g, ln_b, w_out, b_out):
    s, d = x.shape
    row = lambda v: v.reshape(1, -1)
    return pl.pallas_call(
        _conv_kernel,
        grid=(s // ROW_TILE,),
        in_specs=[
            _row_spec(ROW_TILE, d),
            _resident((1, d)),
            _resident((3, d)),
            _resident((d, 2 * d)),
            _resident((1, 2 * d)),
            _resident((CONV_WIDTH, d)),
            _resident((1, d)),
            _resident((1, d)),
            _resident((1, d)),
            _resident((d, d)),
            _resident((1, d)),
        ],
        out_specs=_row_spec(ROW_TILE, d),
        out_shape=jax.ShapeDtypeStruct((s, d), F32),
        scratch_shapes=[pltpu.VMEM((CONV_HALO + ROW_TILE, d), F32), pltpu.VMEM((ROW_TILE, d), F32)],
        compiler_params=_params(("arbitrary",)),
        name="conv_mixer",
    )(x, row(g), mod, w_in.astype(BF16), row(b_in), dw, row(dw_b), row(ln_g), row(ln_b),
      w_out.astype(BF16), row(b_out))


def _qkv_kernel(x_ref, g_ref, mod_ref, w_ref, qg_ref, kg_ref, o_ref):
    tm, d = x_ref.shape
    h = _modulate(x_ref[...], g_ref[...], mod_ref).astype(BF16)
    qkv = jnp.dot(h, w_ref[...], preferred_element_type=F32)
    first = lax.broadcasted_iota(jnp.int32, (1, LANES), 1) < HEAD_DIM

    def head_norm(blk, gain):
        sq = blk * blk
        ms_a = jnp.sum(jnp.where(first, sq, 0.0), axis=-1, keepdims=True) * (1.0 / HEAD_DIM)
        ms_b = jnp.sum(jnp.where(first, 0.0, sq), axis=-1, keepdims=True) * (1.0 / HEAD_DIM)
        r = jnp.where(first, lax.rsqrt(ms_a + EPS), lax.rsqrt(ms_b + EPS))
        return blk * r * gain

    nb = d // LANES
    for j in range(nb):
        cs = slice(j * LANES, (j + 1) * LANES)
        o_ref[:, cs] = head_norm(qkv[:, cs], qg_ref[...]) * (HEAD_DIM ** -0.5)
    for j in range(nb, 2 * nb):
        cs = slice(j * LANES, (j + 1) * LANES)
        o_ref[:, cs] = head_norm(qkv[:, cs], kg_ref[...])
    o_ref[:, 2 * d:] = qkv[:, 2 * d:]


def _qkv_proj(x, g, mod, w_qkv, q_g, k_g):
    s, d = x.shape
    tile_gain = lambda v: jnp.tile(v, HEADS_PER_BLOCK).reshape(1, LANES)
    return pl.pallas_call(
        _qkv_kernel,
        grid=(s // ROW_TILE,),
        in_specs=[
            _row_spec(ROW_TILE, d),
            _resident((1, d)),
            _resident((3, d)),
            _resident((d, 3 * d)),
            _resident((1, LANES)),
            _resident((1, LANES)),
        ],
        out_specs=_row_spec(ROW_TILE, 3 * d),
        out_shape=jax.ShapeDtypeStruct((s, 3 * d), F32),
        compiler_params=_params(("arbitrary",)),
        name="qkv_proj",
    )(x, g.reshape(1, d), mod, w_qkv.astype(BF16), tile_gain(q_g), tile_gain(k_g))


def _dot_nt(a, b):
    return lax.dot_general(a, b, (((1,), (1,)), ((), ())), preferred_element_type=F32)


def _attn_kernel(q_ref, k_ref, v_ref, o_ref, c_ref, acc_ref):
    tq = q_ref.shape[0]
    qb = pl.program_id(1)
    q0 = qb * tq
    first = lax.broadcasted_iota(jnp.int32, (1, LANES), 1) < HEAD_DIM
    t_pos = q0 + lax.broadcasted_iota(jnp.int32, (tq, 1), 0)
    k_iota = lax.broadcasted_iota(jnp.int32, (1, K_TILE), 1)
    upper = (lax.broadcasted_iota(jnp.int32, (K_TILE, K_TILE), 0)
             > lax.broadcasted_iota(jnp.int32, (K_TILE, K_TILE), 1)).astype(BF16)
    q = q_ref[...]
    n_kb = (q0 + tq) // K_TILE

    def one_head(sel):
        q_hi, q_lo = _split_bf16(jnp.where(sel, q, 0.0))
        c_ref[...] = jnp.zeros_like(c_ref)
        acc_ref[...] = jnp.zeros_like(acc_ref)

        def step(i, carry):
            k0 = pl.multiple_of((n_kb - 1 - i) * K_TILE, K_TILE)
            k_hi, k_lo = _split_bf16(k_ref[pl.ds(k0, K_TILE), :])
            v_hi, v_lo = _split_bf16(v_ref[pl.ds(k0, K_TILE), :])
            z = _dot_nt(q_hi, k_hi) + _dot_nt(q_hi, k_lo) + _dot_nt(q_lo, k_hi)
            mask = (k0 + k_iota) < t_pos
            sp = jnp.maximum(z, 0.0) + jnp.log1p(jnp.exp(-jnp.abs(z)))
            spm = jnp.where(mask, sp, 0.0)
            sp_hi, sp_lo = _split_bf16(spm)
            c = c_ref[...]
            suffix = (jnp.dot(sp_hi, upper, preferred_element_type=F32)
                      + jnp.dot(sp_lo, upper, preferred_element_type=F32) + c)
            a = jnp.where(mask, jnp.exp(z - sp - suffix), 0.0).astype(BF16)
            acc_ref[...] += (jnp.dot(a, v_hi, preferred_element_type=F32)
                             + jnp.dot(a, v_lo, preferred_element_type=F32))
            c_ref[...] = c + jnp.sum(spm, axis=-1, keepdims=True)
            return carry

        lax.fori_loop(0, n_kb, step, 0)
        return acc_ref[...]

    o_a = one_head(first)
    o_b = one_head(jnp.logical_not(first))
    o_ref[...] = jnp.where(first, o_a, o_b)


def _attention(qkv, d):
    s = qkv.shape[0]
    nb = d // LANES
    return pl.pallas_call(
        _attn_kernel,
        grid=(nb, s // Q_TILE),
        in_specs=[
            pl.BlockSpec((Q_TILE, LANES), lambda hb, qb: (qb, hb)),
            pl.BlockSpec((s, LANES), lambda hb, qb: (0, nb + hb)),
            pl.BlockSpec((s, LANES), lambda hb, qb: (0, 2 * nb + hb)),
        ],
        out_specs=pl.BlockSpec((Q_TILE, LANES), lambda hb, qb: (qb, hb)),
        out_shape=jax.ShapeDtypeStruct((s, d), F32),
        scratch_shapes=[pltpu.VMEM((Q_TILE, 1), F32), pltpu.VMEM((Q_TILE, LANES), F32)],
        compiler_params=_params(("arbitrary", "arbitrary")),
        name="stickbreak_attn",
    )(qkv, qkv, qkv)


def _oproj_kernel(x_ref, a_ref, mod_ref, w_ref, o_ref):
    y = jnp.dot(a_ref[...].astype(BF16), w_ref[...], preferred_element_type=F32)
    o_ref[...] = x_ref[...] + (1.0 + mod_ref[2:3, :]) * y


def _out_proj(x, attn, mod, w_o):
    s, d = x.shape
    return pl.pallas_call(
        _oproj_kernel,
        grid=(s // ROW_TILE,),
        in_specs=[_row_spec(ROW_TILE, d), _row_spec(ROW_TILE, d), _resident((3, d)), _resident((d, d))],
        out_specs=_row_spec(ROW_TILE, d),
        out_shape=jax.ShapeDtypeStruct((s, d), F32),
        compiler_params=_params(("arbitrary",)),
        name="attn_out_proj",
    )(x, attn, mod, w_o.astype(BF16))


def _pool_kernel(x_ref, g_ref, mod_ref, pw_ref, pb_ref, ps_ref, o_ref, h_ref):
    tm, d = x_ref.shape
    gd = d // len(POOL_WINDOWS)
    i = pl.program_id(0)

    @pl.when(i == 0)
    def _():
        h_ref[0:POOL_HALO, :] = jnp.zeros((POOL_HALO, d), F32)

    @pl.when(i > 0)
    def _():
        h_ref[0:POOL_HALO, :] = h_ref[tm:tm + POOL_HALO, :]

    x = x_ref[...]
    h = _modulate(x, g_ref[...], mod_ref)
    h_ref[POOL_HALO:POOL_HALO + tm, :] = h
    t_pos = i * tm + lax.broadcasted_iota(jnp.int32, (tm, 1), 0)
    gate = 1.0 + mod_ref[2:3, :]
    for gi, w in enumerate(POOL_WINDOWS):
        cs = slice(gi * gd, (gi + 1) * gd)
        win = h[:, cs]
        for j in range(1, w):
            win = win + h_ref[POOL_HALO - j:POOL_HALO - j + tm, cs]
        cnt = jnp.minimum(t_pos + 1, w).astype(F32)
        diff = win / cnt - h[:, cs]
        y = jnp.dot(diff.astype(BF16), pw_ref[gi], preferred_element_type=F32) + pb_ref[gi:gi + 1, :]
        o_ref[:, cs] = x[:, cs] + gate[:, cs] * (y * ps_ref[:, cs])


def _pool_mixer(x, g, mod, p_w, p_b, p_scale):
    s, d = x.shape
    ng, gd, _ = p_w.shape
    return pl.pallas_call(
        _pool_kernel,
        grid=(s // ROW_TILE,),
        in_specs=[
            _row_spec(ROW_TILE, d),
            _resident((1, d)),
            _resident((3, d)),
            _resident((ng, gd, gd)),
            _resident((ng, gd)),
            _resident((1, d)),
        ],
        out_specs=_row_spec(ROW_TILE, d),
        out_shape=jax.ShapeDtypeStruct((s, d), F32),
        scratch_shapes=[pltpu.VMEM((POOL_HALO + ROW_TILE, d), F32)],
        compiler_params=_params(("arbitrary",)),
        name="pool_mixer",
    )(x, g.reshape(1, d), mod, p_w.astype(BF16), p_b, p_scale.reshape(1, d))


def kernel(x, c, cond_w, cond_b, ada_w, ada_b, norm_g, ffn_w1, ffn_w3, ffn_w2, a_w_in, a_b_in, a_dw, a_dw_b, a_ln_g, a_ln_b, a_w_out, a_b_out, b_w_qkv, b_q_g, b_k_g, b_w_o, p_w, p_b, p_scale):
    batch, s, d = x.shape
    assert batch == 1 and d == N_HEADS * HEAD_DIM
    assert s % ROW_TILE == 0 and s % Q_TILE == 0 and Q_TILE % K_TILE == 0
    depth = ada_w.shape[0]
    mod = _conditioning(c, cond_w, cond_b, ada_w, ada_b)
    xs = x.reshape(s, d)
    ia = ib = ic = 0
    for i in range(depth):
        xs = _ffn(xs, norm_g[i, 0], mod[i, 0], ffn_w1[i, 0], ffn_w3[i, 0], ffn_w2[i, 0])
        kind = i % 3
        if kind == 0:
            xs = _conv_mixer(xs, norm_g[i, 1], mod[i, 1], a_w_in[ia], a_b_in[ia], a_dw[ia], a_dw_b[ia],
                             a_ln_g[ia], a_ln_b[ia], a_w_out[ia], a_b_out[ia])
            ia += 1
        elif kind == 1:
            qkv = _qkv_proj(xs, norm_g[i, 1], mod[i, 1], b_w_qkv[ib], b_q_g[ib], b_k_g[ib])
            attn = _attention(qkv, d)
            xs = _out_proj(xs, attn, mod[i, 1], b_w_o[ib])
            ib += 1
        else:
            xs = _pool_mixer(xs, norm_g[i, 1], mod[i, 1], p_w[ic], p_b[ic], p_scale[ic])
            ic += 1
        xs = _ffn(xs, norm_g[i, 2], mod[i, 2], ffn_w1[i, 1], ffn_w3[i, 1], ffn_w2[i, 1])
    return xs.reshape(batch, s, d)
```

```python
import jax
import jax.numpy as jnp
from jax import lax
from jax.experimental import pallas as pl
from jax.experimental.pallas import tpu as pltpu

F32 = jnp.float32
BF16 = jnp.bfloat16

EPS = 1e-6
N_HEADS = 16
HEAD_DIM = 64
CONV_WIDTH = 31
POOL_WINDOWS = (2, 4, 8, 16)
LANES = 128
SUBLANES = 8
HEADS_PER_BLOCK = LANES // HEAD_DIM

ROW_TILE = 512
FF_CHUNK = 256
CONV_HALO = 32
CONV_ROWS = 64
CONV_LANES = 256
POOL_HALO = 16
ATT_TILE = 128
ATT_FAST_BLOCKS = 2
ATT_UNROLL = 6
ATT_ZERO_WEIGHT = 104.0
VMEM_LIMIT = 56 * 1024 * 1024


def _sigmoid(x):
    return 1.0 / (1.0 + jnp.exp(-x))


def _modulate(x, g, mod_ref):
    ms = jnp.mean(x * x, axis=-1, keepdims=True)
    gs = g * (1.0 + mod_ref[1:2, :])
    return (x * lax.rsqrt(ms + EPS)) * gs + mod_ref[0:1, :]


def _resident(shape):
    zeros = (0,) * len(shape)
    return pl.BlockSpec(shape, lambda *_: zeros, pipeline_mode=pl.Buffered(1))


def _row_spec(tile, width):
    return pl.BlockSpec((tile, width), lambda i: (i, 0))


def _params(semantics):
    return pltpu.CompilerParams(dimension_semantics=semantics, vmem_limit_bytes=VMEM_LIMIT)


def _cond_embed_kernel(c_ref, w_ref, b_ref, e_ref):
    t = jnp.sum(c_ref[...] * w_ref[...], axis=0, keepdims=True) + b_ref[...]
    e_ref[...] = t * _sigmoid(t)


def _cond_mod_kernel(e_ref, w_ref, b_ref, o_ref):
    o_ref[0] = jnp.sum(e_ref[...] * w_ref[0], axis=0, keepdims=True) + b_ref[0]


def _conditioning(c, cond_w, cond_b, ada_w, ada_b):
    d = cond_w.shape[0]
    depth, _, n = ada_w.shape
    e = pl.pallas_call(
        _cond_embed_kernel,
        out_shape=jax.ShapeDtypeStruct((1, d), F32),
        name="cond_embed",
    )(c.reshape(d, 1), cond_w, cond_b.reshape(1, d))
    tn = n // 8
    mod = pl.pallas_call(
        _cond_mod_kernel,
        grid=(depth, n // tn),
        in_specs=[
            pl.BlockSpec((d, 1), lambda l, j: (0, 0)),
            pl.BlockSpec((1, d, tn), lambda l, j: (l, 0, j)),
            pl.BlockSpec((1, 1, tn), lambda l, j: (l, 0, j)),
        ],
        out_specs=pl.BlockSpec((1, 1, tn), lambda l, j: (l, 0, j)),
        out_shape=jax.ShapeDtypeStruct((depth, 1, n), F32),
        compiler_params=_params(("arbitrary", "arbitrary")),
        name="cond_mod",
    )(e.reshape(d, 1), ada_w, ada_b.reshape(depth, 1, n))
    return mod.reshape(depth, 3, 3, d)


def _ffn_kernel(x_ref, g_ref, mod_ref, w1_ref, w3_ref, w2_ref, o_ref, h_ref, acc_ref):
    x = x_ref[...]
    h_ref[...] = _modulate(x, g_ref[...], mod_ref).astype(BF16)
    acc_ref[...] = jnp.zeros_like(acc_ref)

    def chunk(f, carry):
        h = h_ref[...]
        a = jnp.dot(h, w1_ref[f], preferred_element_type=F32)
        b = jnp.dot(h, w3_ref[f], preferred_element_type=F32)
        act = (a * _sigmoid(a) * b).astype(BF16)
        acc_ref[...] += jnp.dot(act, w2_ref[f], preferred_element_type=F32)
        return carry

    lax.fori_loop(0, w1_ref.shape[0], chunk, 0, unroll=True)
    o_ref[...] = x + (0.5 * (1.0 + mod_ref[2:3, :])) * acc_ref[...]


def _ffn(x, g, mod, w1, w3, w2):
    s, d = x.shape
    f = w1.shape[1]
    nf = f // FF_CHUNK
    w1c = w1.astype(BF16).reshape(d, nf, FF_CHUNK).transpose(1, 0, 2)
    w3c = w3.astype(BF16).reshape(d, nf, FF_CHUNK).transpose(1, 0, 2)
    w2c = w2.astype(BF16).reshape(nf, FF_CHUNK, d)
    return pl.pallas_call(
        _ffn_kernel,
        grid=(s // ROW_TILE,),
        in_specs=[
            _row_spec(ROW_TILE, d),
            _resident((1, d)),
            _resident((3, d)),
            _resident((nf, d, FF_CHUNK)),
            _resident((nf, d, FF_CHUNK)),
            _resident((nf, FF_CHUNK, d)),
        ],
        out_specs=_row_spec(ROW_TILE, d),
        out_shape=jax.ShapeDtypeStruct((s, d), F32),
        scratch_shapes=[pltpu.VMEM((ROW_TILE, d), BF16), pltpu.VMEM((ROW_TILE, d), F32)],
        compiler_params=_params(("arbitrary",)),
        name="ffn",
    )(x, g.reshape(1, d), mod, w1c, w3c, w2c)


def _conv_kernel(x_ref, g_ref, mod_ref, win_ref, bin_ref, dw_ref, dwb_ref, lng_ref, lnb_ref,
                 wout_ref, bout_ref, o_ref, u_ref, us_ref, v_ref):
    tm, d = x_ref.shape
    i = pl.program_id(0)

    @pl.when(i == 0)
    def _():
        u_ref[0:CONV_HALO, :] = jnp.zeros((CONV_HALO, d), F32)

    @pl.when(i > 0)
    def _():
        u_ref[0:CONV_HALO, :] = u_ref[tm:tm + CONV_HALO, :]

    x = x_ref[...]
    h = _modulate(x, g_ref[...], mod_ref).astype(BF16)
    uv = jnp.dot(h, win_ref[...], preferred_element_type=F32) + bin_ref[...]
    u_ref[CONV_HALO:CONV_HALO + tm, :] = uv[:, :d] * _sigmoid(uv[:, d:])

    n_shift = us_ref.shape[1]
    for b in range(1, SUBLANES):
        us_ref[b - 1] = u_ref[b:b + n_shift, :]

    base = CONV_HALO - (CONV_WIDTH - 1)

    def rows(r, carry):
        r0 = pl.multiple_of(r * CONV_ROWS, CONV_ROWS)
        for c in range(d // CONV_LANES):
            cs = slice(c * CONV_LANES, (c + 1) * CONV_LANES)
            acc = jnp.broadcast_to(dwb_ref[:, cs], (CONV_ROWS, CONV_LANES))
            for k in range(CONV_WIDTH):
                a, b = divmod(base + k, SUBLANES)
                rs = pl.ds(r0 + a * SUBLANES, CONV_ROWS)
                src = u_ref[rs, cs] if b == 0 else us_ref[b - 1, rs, cs]
                acc = acc + dw_ref[k:k + 1, cs] * src
            v_ref[pl.ds(r0, CONV_ROWS), cs] = acc
        return carry

    lax.fori_loop(0, tm // CONV_ROWS, rows, 0)

    v = v_ref[...]
    mu = jnp.mean(v, axis=-1, keepdims=True)
    vc = v - mu
    var = jnp.mean(vc * vc, axis=-1, keepdims=True)
    y = vc * lax.rsqrt(var + EPS) * lng_ref[...] + lnb_ref[...]
    y = (y * _sigmoid(y)).astype(BF16)
    y = jnp.dot(y, wout_ref[...], preferred_element_type=F32) + bout_ref[...]
    o_ref[...] = x + (1.0 + mod_ref[2:3, :]) * y


def _conv_mixer(x, g, mod, w_in, b_in, dw, dw_b, ln_g, ln_b, w_out, b_out):
    s, d = x.shape
    row = lambda v: v.reshape(1, -1)
    n_shift = ROW_TILE + CONV_HALO - SUBLANES
    return pl.pallas_call(
        _conv_kernel,
        grid=(s // ROW_TILE,),
        in_specs=[
            _row_spec(ROW_TILE, d),
            _resident((1, d)),
            _resident((3, d)),
            _resident((d, 2 * d)),
            _resident((1, 2 * d)),
            _resident((CONV_WIDTH, d)),
            _resident((1, d)),
            _resident((1, d)),
            _resident((1, d)),
            _resident((d, d)),
            _resident((1, d)),
        ],
        out_specs=_row_spec(ROW_TILE, d),
        out_shape=jax.ShapeDtypeStruct((s, d), F32),
        scratch_shapes=[
            pltpu.VMEM((CONV_HALO + ROW_TILE, d), F32),
            pltpu.VMEM((SUBLANES - 1, n_shift, d), F32),
            pltpu.VMEM((ROW_TILE, d), F32),
        ],
        compiler_params=_params(("arbitrary",)),
        name="conv_mixer",
    )(x, row(g), mod, w_in.astype(BF16), row(b_in), dw, row(dw_b), row(ln_g), row(ln_b),
      w_out.astype(BF16), row(b_out))


def _qkv_kernel(x_ref, g_ref, mod_ref, w_ref, qg_ref, kg_ref, o_ref):
    tm, d = x_ref.shape
    h = _modulate(x_ref[...], g_ref[...], mod_ref).astype(BF16)
    qkv = jnp.dot(h, w_ref[...], preferred_element_type=F32)
    first = lax.broadcasted_iota(jnp.int32, (1, LANES), 1) < HEAD_DIM

    def head_norm(blk, gain):
        sq = blk * blk
        ms_a = jnp.sum(jnp.where(first, sq, 0.0), axis=-1, keepdims=True) * (1.0 / HEAD_DIM)
        ms_b = jnp.sum(jnp.where(first, 0.0, sq), axis=-1, keepdims=True) * (1.0 / HEAD_DIM)
        r = jnp.where(first, lax.rsqrt(ms_a + EPS), lax.rsqrt(ms_b + EPS))
        return blk * r * gain

    nb = d // LANES
    for j in range(nb):
        cs = slice(j * LANES, (j + 1) * LANES)
        o_ref[:, cs] = (head_norm(qkv[:, cs], qg_ref[...]) * (HEAD_DIM ** -0.5)).astype(BF16)
    for j in range(nb, 2 * nb):
        cs = slice(j * LANES, (j + 1) * LANES)
        o_ref[:, cs] = head_norm(qkv[:, cs], kg_ref[...]).astype(BF16)
    o_ref[:, 2 * d:] = qkv[:, 2 * d:].astype(BF16)


def _qkv_proj(x, g, mod, w_qkv, q_g, k_g):
    s, d = x.shape
    tile_gain = lambda v: jnp.tile(v, HEADS_PER_BLOCK).reshape(1, LANES)
    return pl.pallas_call(
        _qkv_kernel,
        grid=(s // ROW_TILE,),
        in_specs=[
            _row_spec(ROW_TILE, d),
            _resident((1, d)),
            _resident((3, d)),
            _resident((d, 3 * d)),
            _resident((1, LANES)),
            _resident((1, LANES)),
        ],
        out_specs=_row_spec(ROW_TILE, 3 * d),
        out_shape=jax.ShapeDtypeStruct((s, 3 * d), BF16),
        compiler_params=_params(("arbitrary",)),
        name="qkv_proj",
    )(x, g.reshape(1, d), mod, w_qkv.astype(BF16), tile_gain(q_g), tile_gain(k_g))


def _attn_kernel(q_ref, k_ref, v_ref, o_ref, c_ref, acc_ref):
    n = ATT_TILE
    s = q_ref.shape[0]
    first = lax.broadcasted_iota(jnp.int32, (1, LANES), 1) < HEAD_DIM
    row = lax.broadcasted_iota(jnp.int32, (n, n), 0)
    col = lax.broadcasted_iota(jnp.int32, (n, n), 1)
    later_and_total = jnp.concatenate([(row > col).astype(BF16), jnp.ones((n, n), BF16)], axis=1)
    strictly_causal = jnp.concatenate([col < row, col < row], axis=1)
    zero = jnp.zeros((), BF16)

    def key_blocks(jobs, c, acc):
        k2, v2 = {}, {}
        for _, _, kb, _ in jobs:
            if id(kb) not in k2:
                k0 = pl.multiple_of(kb * n, n)
                k = k_ref[pl.ds(k0, n), :]
                v = v_ref[pl.ds(k0, n), :]
                k2[id(kb)] = jnp.concatenate([jnp.where(first, k, zero), jnp.where(first, zero, k)], axis=0)
                v2[id(kb)] = jnp.concatenate([jnp.where(first, v, zero), jnp.where(first, zero, v)], axis=0)
        z = [lax.dot_general(q, k2[id(kb)], (((1,), (1,)), ((), ())), preferred_element_type=F32)
             for _, q, kb, _ in jobs]
        sp = [jnp.maximum(zi, 0.0) + jnp.log(1.0 + jnp.exp(-jnp.abs(zi))) for zi in z]
        spb = [(jnp.where(strictly_causal, si, 0.0) if job[3] else si).astype(BF16) for si, job in zip(sp, jobs)]
        run_a = [jnp.dot(si[:, :n], later_and_total, preferred_element_type=F32) for si in spb]
        run_b = [jnp.dot(si[:, n:], later_and_total, preferred_element_type=F32) for si in spb]
        a = []
        for i, (u, _, _, diagonal) in enumerate(jobs):
            suffix = jnp.concatenate([run_a[i][:, :n], run_b[i][:, :n]], axis=1) + c[u]
            c[u] = c[u] + jnp.concatenate([run_a[i][:, n:], run_b[i][:, n:]], axis=1)
            ai = jnp.exp(z[i] - sp[i] - suffix)
            if diagonal:
                ai = jnp.where(strictly_causal, ai, 0.0)
            a.append(ai.astype(BF16))
        for i, (u, _, kb, _) in enumerate(jobs):
            acc[u] = acc[u] + jnp.dot(a[i], v2[id(kb)], preferred_element_type=F32)

    def key_block(q, kb, diagonal, c, acc):
        c, acc = [c], [acc]
        key_blocks([(0, q, kb, diagonal)], c, acc)
        return c[0], acc[0]

    def query_tiles(base, n_tiles, n_fast):
        blocks = {off: base + off for off in range(-n_fast, n_tiles)}
        jobs = []
        for u in range(n_tiles):
            q = q_ref[pl.ds(pl.multiple_of(blocks[u] * n, n), n), :]
            jobs += [(u, q, blocks[u - j], j == 0) for j in range(n_fast + 1)]
        c = [jnp.zeros((n, 2 * n), F32)] * n_tiles
        acc = [jnp.zeros((n, LANES), F32)] * n_tiles
        key_blocks(jobs, c, acc)
        c_all = None
        for u in range(n_tiles):
            c_ref[u] = c[u]
            acc_ref[u] = acc[u]
            c_all = c[u] if c_all is None else jnp.minimum(c_all, c[u])

        @pl.when(jnp.min(c_all) < ATT_ZERO_WEIGHT)
        def _():
            for u in range(n_tiles):
                qb = base + u
                q = q_ref[pl.ds(pl.multiple_of(qb * n, n), n), :]

                def more(carry):
                    kb, c_min = carry
                    return jnp.logical_and(kb >= 0, c_min < ATT_ZERO_WEIGHT)

                def step(carry, u=u, q=q):
                    kb, _ = carry
                    c, acc = key_block(q, kb, False, c_ref[u], acc_ref[u])
                    c_ref[u] = c
                    acc_ref[u] = acc
                    return kb - 1, jnp.min(c)

                lax.while_loop(more, step, (qb - n_fast - 1, jnp.min(c_ref[u])))

        for u in range(n_tiles):
            q0 = pl.multiple_of((base + u) * n, n)
            o_ref[pl.ds(q0, n), :] = acc_ref[u].astype(o_ref.dtype)

    def head_tile(qb, carry):
        query_tiles(qb, 1, 0)
        return carry

    def body_tiles(g, carry):
        query_tiles(n_head + g * ATT_UNROLL, ATT_UNROLL, ATT_FAST_BLOCKS)
        return carry

    n_total = s // n
    n_head = ATT_FAST_BLOCKS + (n_total - ATT_FAST_BLOCKS) % ATT_UNROLL
    lax.fori_loop(0, n_head, head_tile, 0)
    lax.fori_loop(0, (n_total - n_head) // ATT_UNROLL, body_tiles, 0)


def _attention(qkv, d):
    s = qkv.shape[0]
    nb = d // LANES
    return pl.pallas_call(
        _attn_kernel,
        grid=(nb,),
        in_specs=[
            pl.BlockSpec((s, LANES), lambda hb: (0, hb)),
            pl.BlockSpec((s, LANES), lambda hb: (0, nb + hb)),
            pl.BlockSpec((s, LANES), lambda hb: (0, 2 * nb + hb)),
        ],
        out_specs=pl.BlockSpec((s, LANES), lambda hb: (0, hb)),
        out_shape=jax.ShapeDtypeStruct((s, d), BF16),
        scratch_shapes=[pltpu.VMEM((ATT_UNROLL, ATT_TILE, 2 * ATT_TILE), F32),
                        pltpu.VMEM((ATT_UNROLL, ATT_TILE, LANES), F32)],
        compiler_params=_params(("arbitrary",)),
        name="stickbreak_attn",
    )(qkv, qkv, qkv)


def _oproj_kernel(x_ref, a_ref, mod_ref, w_ref, o_ref):
    y = jnp.dot(a_ref[...], w_ref[...], preferred_element_type=F32)
    o_ref[...] = x_ref[...] + (1.0 + mod_ref[2:3, :]) * y


def _out_proj(x, attn, mod, w_o):
    s, d = x.shape
    return pl.pallas_call(
        _oproj_kernel,
        grid=(s // ROW_TILE,),
        in_specs=[_row_spec(ROW_TILE, d), _row_spec(ROW_TILE, d), _resident((3, d)), _resident((d, d))],
        out_specs=_row_spec(ROW_TILE, d),
        out_shape=jax.ShapeDtypeStruct((s, d), F32),
        compiler_params=_params(("arbitrary",)),
        name="attn_out_proj",
    )(x, attn, mod, w_o.astype(BF16))


def _pool_kernel(x_ref, g_ref, mod_ref, pw_ref, pb_ref, ps_ref, o_ref, h_ref):
    tm, d = x_ref.shape
    gd = d // len(POOL_WINDOWS)
    i = pl.program_id(0)

    @pl.when(i == 0)
    def _():
        h_ref[0:POOL_HALO, :] = jnp.zeros((POOL_HALO, d), F32)

    @pl.when(i > 0)
    def _():
        h_ref[0:POOL_HALO, :] = h_ref[tm:tm + POOL_HALO, :]

    x = x_ref[...]
    h = _modulate(x, g_ref[...], mod_ref)
    h_ref[POOL_HALO:POOL_HALO + tm, :] = h
    t_pos = i * tm + lax.broadcasted_iota(jnp.int32, (tm, 1), 0)
    gate = 1.0 + mod_ref[2:3, :]
    for gi, w in enumerate(POOL_WINDOWS):
        cs = slice(gi * gd, (gi + 1) * gd)
        win = h[:, cs]
        for j in range(1, w):
            win = win + h_ref[POOL_HALO - j:POOL_HALO - j + tm, cs]
        cnt = jnp.minimum(t_pos + 1, w).astype(F32)
        diff = win / cnt - h[:, cs]
        y = jnp.dot(diff.astype(BF16), pw_ref[gi], preferred_element_type=F32) + pb_ref[gi:gi + 1, :]
        o_ref[:, cs] = x[:, cs] + gate[:, cs] * (y * ps_ref[:, cs])


def _pool_mixer(x, g, mod, p_w, p_b, p_scale):
    s, d = x.shape
    ng, gd, _ = p_w.shape
    return pl.pallas_call(
        _pool_kernel,
        grid=(s // ROW_TILE,),
        in_specs=[
            _row_spec(ROW_TILE, d),
            _resident((1, d)),
            _resident((3, d)),
            _resident((ng, gd, gd)),
            _resident((ng, gd)),
            _resident((1, d)),
        ],
        out_specs=_row_spec(ROW_TILE, d),
        out_shape=jax.ShapeDtypeStruct((s, d), F32),
        scratch_shapes=[pltpu.VMEM((POOL_HALO + ROW_TILE, d), F32)],
        compiler_params=_params(("arbitrary",)),
        name="pool_mixer",
    )(x, g.reshape(1, d), mod, p_w.astype(BF16), p_b, p_scale.reshape(1, d))


def kernel(x, c, cond_w, cond_b, ada_w, ada_b, norm_g, ffn_w1, ffn_w3, ffn_w2, a_w_in, a_b_in, a_dw, a_dw_b, a_ln_g, a_ln_b, a_w_out, a_b_out, b_w_qkv, b_q_g, b_k_g, b_w_o, p_w, p_b, p_scale):
    batch, s, d = x.shape
    assert batch == 1 and d == N_HEADS * HEAD_DIM
    assert s % ROW_TILE == 0 and s % ATT_TILE == 0 and s // ATT_TILE >= ATT_FAST_BLOCKS + ATT_UNROLL
    depth = ada_w.shape[0]
    mod = _conditioning(c, cond_w, cond_b, ada_w, ada_b)
    xs = x.reshape(s, d)
    ia = ib = ic = 0
    for i in range(depth):
        xs = _ffn(xs, norm_g[i, 0], mod[i, 0], ffn_w1[i, 0], ffn_w3[i, 0], ffn_w2[i, 0])
        kind = i % 3
        if kind == 0:
            xs = _conv_mixer(xs, norm_g[i, 1], mod[i, 1], a_w_in[ia], a_b_in[ia], a_dw[ia], a_dw_b[ia],
                             a_ln_g[ia], a_ln_b[ia], a_w_out[ia], a_b_out[ia])
            ia += 1
        elif kind == 1:
            qkv = _qkv_proj(xs, norm_g[i, 1], mod[i, 1], b_w_qkv[ib], b_q_g[ib], b_k_g[ib])
            attn = _attention(qkv, d)
            xs = _out_proj(xs, attn, mod[i, 1], b_w_o[ib])
            ib += 1
        else:
            xs = _pool_mixer(xs, norm_g[i, 1], mod[i, 1], p_w[ic], p_b[ic], p_scale[ic])
            ic += 1
        xs = _ffn(xs, norm_g[i, 2], mod[i, 2], ffn_w1[i, 1], ffn_w3[i, 1], ffn_w2[i, 1])
    return xs.reshape(batch, s, d)
```

```python
import jax
import jax.numpy as jnp
from jax import lax
from jax.experimental import pallas as pl
from jax.experimental.pallas import tpu as pltpu

F32 = jnp.float32
BF16 = jnp.bfloat16

EPS = 1e-6
N_HEADS = 16
HEAD_DIM = 64
CONV_WIDTH = 31
POOL_WINDOWS = (2, 4, 8, 16)
LANES = 128
SUBLANES = 8
HEADS_PER_BLOCK = LANES // HEAD_DIM

ROW_TILE = 512
FF_CHUNK = 256
CONV_HALO = 32
CONV_SUB = 128
POOL_HALO = 16
ATT_TILE = 128
ATT_FAST_BLOCKS = 2
ATT_UNROLL = 6
ATT_ZERO_WEIGHT = 104.0
VMEM_LIMIT = 56 * 1024 * 1024


def _sigmoid(x):
    return 1.0 / (1.0 + jnp.exp(-x))


def _modulate(x, g, mod_ref):
    ms = jnp.mean(x * x, axis=-1, keepdims=True)
    gs = g * (1.0 + mod_ref[1:2, :])
    return (x * lax.rsqrt(ms + EPS)) * gs + mod_ref[0:1, :]


def _resident(shape):
    zeros = (0,) * len(shape)
    return pl.BlockSpec(shape, lambda *_: zeros, pipeline_mode=pl.Buffered(1))


def _row_spec(tile, width):
    return pl.BlockSpec((tile, width), lambda i: (i, 0))


def _params(semantics):
    return pltpu.CompilerParams(dimension_semantics=semantics, vmem_limit_bytes=VMEM_LIMIT)


def _cond_embed_kernel(c_ref, w_ref, b_ref, e_ref):
    t = jnp.sum(c_ref[...] * w_ref[...], axis=0, keepdims=True) + b_ref[...]
    e_ref[...] = t * _sigmoid(t)


def _cond_mod_kernel(e_ref, w_ref, b_ref, o_ref):
    o_ref[0] = jnp.sum(e_ref[...] * w_ref[0], axis=0, keepdims=True) + b_ref[0]


def _conditioning(c, cond_w, cond_b, ada_w, ada_b):
    d = cond_w.shape[0]
    depth, _, n = ada_w.shape
    e = pl.pallas_call(
        _cond_embed_kernel,
        out_shape=jax.ShapeDtypeStruct((1, d), F32),
        name="cond_embed",
    )(c.reshape(d, 1), cond_w, cond_b.reshape(1, d))
    tn = n // 8
    mod = pl.pallas_call(
        _cond_mod_kernel,
        grid=(depth, n // tn),
        in_specs=[
            pl.BlockSpec((d, 1), lambda l, j: (0, 0)),
            pl.BlockSpec((1, d, tn), lambda l, j: (l, 0, j)),
            pl.BlockSpec((1, 1, tn), lambda l, j: (l, 0, j)),
        ],
        out_specs=pl.BlockSpec((1, 1, tn), lambda l, j: (l, 0, j)),
        out_shape=jax.ShapeDtypeStruct((depth, 1, n), F32),
        compiler_params=_params(("arbitrary", "arbitrary")),
        name="cond_mod",
    )(e.reshape(d, 1), ada_w, ada_b.reshape(depth, 1, n))
    return mod.reshape(depth, 3, 3, d)


def _ffn_kernel(x_ref, g_ref, mod_ref, w1_ref, w3_ref, w2_ref, o_ref, h_ref, acc_ref):
    x = x_ref[...]
    h_ref[...] = _modulate(x, g_ref[...], mod_ref).astype(BF16)
    acc_ref[...] = jnp.zeros_like(acc_ref)

    for f in range(w1_ref.shape[1] // FF_CHUNK):
        fs = slice(f * FF_CHUNK, (f + 1) * FF_CHUNK)
        h = h_ref[...]
        a = jnp.dot(h, w1_ref[:, fs], preferred_element_type=F32)
        b = jnp.dot(h, w3_ref[:, fs], preferred_element_type=F32)
        act = (a * _sigmoid(a) * b).astype(BF16)
        acc_ref[...] += jnp.dot(act, w2_ref[fs, :], preferred_element_type=F32)
    o_ref[...] = x + (0.5 * (1.0 + mod_ref[2:3, :])) * acc_ref[...]


def _ffn(x, g, mod, w1, w3, w2):
    s, d = x.shape
    f = w1.shape[1]
    assert f % FF_CHUNK == 0
    return pl.pallas_call(
        _ffn_kernel,
        grid=(s // ROW_TILE,),
        in_specs=[
            _row_spec(ROW_TILE, d),
            _resident((1, d)),
            _resident((3, d)),
            _resident((d, f)),
            _resident((d, f)),
            _resident((f, d)),
        ],
        out_specs=_row_spec(ROW_TILE, d),
        out_shape=jax.ShapeDtypeStruct((s, d), F32),
        scratch_shapes=[pltpu.VMEM((ROW_TILE, d), BF16), pltpu.VMEM((ROW_TILE, d), F32)],
        compiler_params=_params(("arbitrary",)),
        name="ffn",
    )(x, g.reshape(1, d), mod, w1.astype(BF16), w3.astype(BF16), w2.astype(BF16))


def _conv_kernel(x_ref, g_ref, mod_ref, win_ref, bin_ref, dw_ref, dwb_ref, lng_ref, lnb_ref,
                 wout_ref, bout_ref, o_ref, u_ref, us_ref, v_ref):
    tm, d = x_ref.shape
    i = pl.program_id(0)

    @pl.when(i == 0)
    def _():
        u_ref[0:CONV_HALO, :] = jnp.zeros((CONV_HALO, d), F32)

    @pl.when(i > 0)
    def _():
        u_ref[0:CONV_HALO, :] = u_ref[tm:tm + CONV_HALO, :]

    gate = 1.0 + mod_ref[2:3, :]
    base = CONV_HALO - (CONV_WIDTH - 1)

    def glu_in(k):
        rs = slice(k * CONV_SUB, (k + 1) * CONV_SUB)
        h = _modulate(x_ref[rs, :], g_ref[...], mod_ref).astype(BF16)
        uv = jnp.dot(h, win_ref[...], preferred_element_type=F32) + bin_ref[...]
        u_ref[CONV_HALO + k * CONV_SUB:CONV_HALO + (k + 1) * CONV_SUB, :] = uv[:, :d] * _sigmoid(uv[:, d:])

    def depthwise(k):
        lo = k * CONV_SUB
        for b in range(1, SUBLANES):
            us_ref[b - 1] = u_ref[lo + b:lo + b + us_ref.shape[1], :]
        for c in range(d // LANES):
            cs = slice(c * LANES, (c + 1) * LANES)
            taps = [jnp.broadcast_to(dw_ref[j:j + 1, cs], (SUBLANES, LANES)) for j in range(CONV_WIDTH)]
            bias = jnp.broadcast_to(dwb_ref[:, cs], (SUBLANES, LANES))
            for r in range(0, CONV_SUB, SUBLANES):
                acc = bias
                for j in range(CONV_WIDTH):
                    a, b = divmod(base + j, SUBLANES)
                    r0 = r + a * SUBLANES
                    src = u_ref[lo + r0:lo + r0 + SUBLANES, cs] if b == 0 else us_ref[b - 1, r0:r0 + SUBLANES, cs]
                    acc = acc + taps[j] * src
                v_ref[lo + r:lo + r + SUBLANES, cs] = acc

    def norm_out(k):
        rs = slice(k * CONV_SUB, (k + 1) * CONV_SUB)
        v = v_ref[rs, :]
        mu = jnp.mean(v, axis=-1, keepdims=True)
        vc = v - mu
        var = jnp.mean(vc * vc, axis=-1, keepdims=True)
        y = vc * lax.rsqrt(var + EPS) * lng_ref[...] + lnb_ref[...]
        y = (y * _sigmoid(y)).astype(BF16)
        y = jnp.dot(y, wout_ref[...], preferred_element_type=F32) + bout_ref[...]
        o_ref[rs, :] = x_ref[rs, :] + gate * y

    n_sub = tm // CONV_SUB
    glu_in(0)
    for k in range(n_sub):
        if k + 1 < n_sub:
            glu_in(k + 1)
        depthwise(k)
        norm_out(k)


def _conv_mixer(x, g, mod, w_in, b_in, dw, dw_b, ln_g, ln_b, w_out, b_out):
    s, d = x.shape
    row = lambda v: v.reshape(1, -1)
    return pl.pallas_call(
        _conv_kernel,
        grid=(s // ROW_TILE,),
        in_specs=[
            _row_spec(ROW_TILE, d),
            _resident((1, d)),
            _resident((3, d)),
            _resident((d, 2 * d)),
            _resident((1, 2 * d)),
            _resident((CONV_WIDTH, d)),
            _resident((1, d)),
            _resident((1, d)),
            _resident((1, d)),
            _resident((d, d)),
            _resident((1, d)),
        ],
        out_specs=_row_spec(ROW_TILE, d),
        out_shape=jax.ShapeDtypeStruct((s, d), F32),
        scratch_shapes=[
            pltpu.VMEM((CONV_HALO + ROW_TILE, d), F32),
            pltpu.VMEM((SUBLANES - 1, CONV_SUB + CONV_HALO - SUBLANES, d), F32),
            pltpu.VMEM((ROW_TILE, d), F32),
        ],
        compiler_params=_params(("arbitrary",)),
        name="conv_mixer",
    )(x, row(g), mod, w_in.astype(BF16), row(b_in), dw, row(dw_b), row(ln_g), row(ln_b),
      w_out.astype(BF16), row(b_out))


def _qkv_kernel(x_ref, g_ref, mod_ref, w_ref, qg_ref, kg_ref, o_ref):
    tm, d = x_ref.shape
    h = _modulate(x_ref[...], g_ref[...], mod_ref).astype(BF16)
    qkv = jnp.dot(h, w_ref[...], preferred_element_type=F32)
    first = lax.broadcasted_iota(jnp.int32, (1, LANES), 1) < HEAD_DIM

    def head_norm(blk, gain):
        sq = blk * blk
        ms_a = jnp.sum(jnp.where(first, sq, 0.0), axis=-1, keepdims=True) * (1.0 / HEAD_DIM)
        ms_b = jnp.sum(jnp.where(first, 0.0, sq), axis=-1, keepdims=True) * (1.0 / HEAD_DIM)
        r = jnp.where(first, lax.rsqrt(ms_a + EPS), lax.rsqrt(ms_b + EPS))
        return blk * r * gain

    nb = d // LANES
    for j in range(nb):
        cs = slice(j * LANES, (j + 1) * LANES)
        o_ref[:, cs] = (head_norm(qkv[:, cs], qg_ref[...]) * (HEAD_DIM ** -0.5)).astype(BF16)
    for j in range(nb, 2 * nb):
        cs = slice(j * LANES, (j + 1) * LANES)
        o_ref[:, cs] = head_norm(qkv[:, cs], kg_ref[...]).astype(BF16)
    o_ref[:, 2 * d:] = qkv[:, 2 * d:].astype(BF16)


def _qkv_proj(x, g, mod, w_qkv, q_g, k_g):
    s, d = x.shape
    tile_gain = lambda v: jnp.tile(v, HEADS_PER_BLOCK).reshape(1, LANES)
    return pl.pallas_call(
        _qkv_kernel,
        grid=(s // ROW_TILE,),
        in_specs=[
            _row_spec(ROW_TILE, d),
            _resident((1, d)),
            _resident((3, d)),
            _resident((d, 3 * d)),
            _resident((1, LANES)),
            _resident((1, LANES)),
        ],
        out_specs=_row_spec(ROW_TILE, 3 * d),
        out_shape=jax.ShapeDtypeStruct((s, 3 * d), BF16),
        compiler_params=_params(("arbitrary",)),
        name="qkv_proj",
    )(x, g.reshape(1, d), mod, w_qkv.astype(BF16), tile_gain(q_g), tile_gain(k_g))


def _attn_kernel(q_ref, k_ref, v_ref, o_ref, c_ref, acc_ref):
    n = ATT_TILE
    s = q_ref.shape[0]
    first = lax.broadcasted_iota(jnp.int32, (1, LANES), 1) < HEAD_DIM
    row = lax.broadcasted_iota(jnp.int32, (n, n), 0)
    col = lax.broadcasted_iota(jnp.int32, (n, n), 1)
    later_and_total = jnp.concatenate([(row > col).astype(BF16), jnp.ones((n, n), BF16)], axis=1)
    strictly_causal = jnp.concatenate([col < row, col < row], axis=1)
    zero = jnp.zeros((), BF16)

    def key_blocks(jobs, c, acc):
        k2, v2 = {}, {}
        for _, _, kb, _ in jobs:
            if id(kb) not in k2:
                k0 = pl.multiple_of(kb * n, n)
                k = k_ref[pl.ds(k0, n), :]
                v = v_ref[pl.ds(k0, n), :]
                k2[id(kb)] = jnp.concatenate([jnp.where(first, k, zero), jnp.where(first, zero, k)], axis=0)
                v2[id(kb)] = jnp.concatenate([jnp.where(first, v, zero), jnp.where(first, zero, v)], axis=0)
        z = [lax.dot_general(q, k2[id(kb)], (((1,), (1,)), ((), ())), preferred_element_type=F32)
             for _, q, kb, _ in jobs]
        sp = [jnp.maximum(zi, 0.0) + jnp.log(1.0 + jnp.exp(-jnp.abs(zi))) for zi in z]
        spb = [(jnp.where(strictly_causal, si, 0.0) if job[3] else si).astype(BF16) for si, job in zip(sp, jobs)]
        run_a = [jnp.dot(si[:, :n], later_and_total, preferred_element_type=F32) for si in spb]
        run_b = [jnp.dot(si[:, n:], later_and_total, preferred_element_type=F32) for si in spb]
        a = []
        for i, (u, _, _, diagonal) in enumerate(jobs):
            suffix = jnp.concatenate([run_a[i][:, :n], run_b[i][:, :n]], axis=1) + c[u]
            c[u] = c[u] + jnp.concatenate([run_a[i][:, n:], run_b[i][:, n:]], axis=1)
            ai = jnp.exp(z[i] - sp[i] - suffix)
            if diagonal:
                ai = jnp.where(strictly_causal, ai, 0.0)
            a.append(ai.astype(BF16))
        for i, (u, _, kb, _) in enumerate(jobs):
            acc[u] = acc[u] + jnp.dot(a[i], v2[id(kb)], preferred_element_type=F32)

    def key_block(q, kb, diagonal, c, acc):
        c, acc = [c], [acc]
        key_blocks([(0, q, kb, diagonal)], c, acc)
        return c[0], acc[0]

    def query_tiles(base, n_tiles, n_fast):
        blocks = {off: base + off for off in range(-n_fast, n_tiles)}
        jobs = []
        for u in range(n_tiles):
            q = q_ref[pl.ds(pl.multiple_of(blocks[u] * n, n), n), :]
            jobs += [(u, q, blocks[u - j], j == 0) for j in range(n_fast + 1)]
        c = [jnp.zeros((n, 2 * n), F32)] * n_tiles
        acc = [jnp.zeros((n, LANES), F32)] * n_tiles
        key_blocks(jobs, c, acc)
        c_all = None
        for u in range(n_tiles):
            c_ref[u] = c[u]
            acc_ref[u] = acc[u]
            c_all = c[u] if c_all is None else jnp.minimum(c_all, c[u])

        @pl.when(jnp.min(c_all) < ATT_ZERO_WEIGHT)
        def _():
            for u in range(n_tiles):
                qb = base + u
                q = q_ref[pl.ds(pl.multiple_of(qb * n, n), n), :]

                def more(carry):
                    kb, c_min = carry
                    return jnp.logical_and(kb >= 0, c_min < ATT_ZERO_WEIGHT)

                def step(carry, u=u, q=q):
                    kb, _ = carry
                    c, acc = key_block(q, kb, False, c_ref[u], acc_ref[u])
                    c_ref[u] = c
                    acc_ref[u] = acc
                    return kb - 1, jnp.min(c)

                lax.while_loop(more, step, (qb - n_fast - 1, jnp.min(c_ref[u])))

        for u in range(n_tiles):
            q0 = pl.multiple_of((base + u) * n, n)
            o_ref[pl.ds(q0, n), :] = acc_ref[u].astype(o_ref.dtype)

    def head_tile(qb, carry):
        query_tiles(qb, 1, 0)
        return carry

    def body_tiles(g, carry):
        query_tiles(n_head + g * ATT_UNROLL, ATT_UNROLL, ATT_FAST_BLOCKS)
        return carry

    n_total = s // n
    n_head = ATT_FAST_BLOCKS + (n_total - ATT_FAST_BLOCKS) % ATT_UNROLL
    lax.fori_loop(0, n_head, head_tile, 0)
    lax.fori_loop(0, (n_total - n_head) // ATT_UNROLL, body_tiles, 0)


def _attention(qkv, d):
    s = qkv.shape[0]
    nb = d // LANES
    return pl.pallas_call(
        _attn_kernel,
        grid=(nb,),
        in_specs=[
            pl.BlockSpec((s, LANES), lambda hb: (0, hb)),
            pl.BlockSpec((s, LANES), lambda hb: (0, nb + hb)),
            pl.BlockSpec((s, LANES), lambda hb: (0, 2 * nb + hb)),
        ],
        out_specs=pl.BlockSpec((s, LANES), lambda hb: (0, hb)),
        out_shape=jax.ShapeDtypeStruct((s, d), BF16),
        scratch_shapes=[pltpu.VMEM((ATT_UNROLL, ATT_TILE, 2 * ATT_TILE), F32),
                        pltpu.VMEM((ATT_UNROLL, ATT_TILE, LANES), F32)],
        compiler_params=_params(("arbitrary",)),
        name="stickbreak_attn",
    )(qkv, qkv, qkv)


def _oproj_kernel(x_ref, a_ref, mod_ref, w_ref, o_ref):
    y = jnp.dot(a_ref[...], w_ref[...], preferred_element_type=F32)
    o_ref[...] = x_ref[...] + (1.0 + mod_ref[2:3, :]) * y


def _out_proj(x, attn, mod, w_o):
    s, d = x.shape
    return pl.pallas_call(
        _oproj_kernel,
        grid=(s // ROW_TILE,),
        in_specs=[_row_spec(ROW_TILE, d), _row_spec(ROW_TILE, d), _resident((3, d)), _resident((d, d))],
        out_specs=_row_spec(ROW_TILE, d),
        out_shape=jax.ShapeDtypeStruct((s, d), F32),
        compiler_params=_params(("arbitrary",)),
        name="attn_out_proj",
    )(x, attn, mod, w_o.astype(BF16))


def _pool_kernel(x_ref, g_ref, mod_ref, pw_ref, pb_ref, ps_ref, o_ref, h_ref):
    tm, d = x_ref.shape
    gd = d // len(POOL_WINDOWS)
    i = pl.program_id(0)

    @pl.when(i == 0)
    def _():
        h_ref[0:POOL_HALO, :] = jnp.zeros((POOL_HALO, d), F32)

    @pl.when(i > 0)
    def _():
        h_ref[0:POOL_HALO, :] = h_ref[tm:tm + POOL_HALO, :]

    x = x_ref[...]
    h = _modulate(x, g_ref[...], mod_ref)
    h_ref[POOL_HALO:POOL_HALO + tm, :] = h
    t_pos = i * tm + lax.broadcasted_iota(jnp.int32, (tm, 1), 0)
    gate = 1.0 + mod_ref[2:3, :]
    for gi, w in enumerate(POOL_WINDOWS):
        cs = slice(gi * gd, (gi + 1) * gd)
        win = h[:, cs]
        for j in range(1, w):
            win = win + h_ref[POOL_HALO - j:POOL_HALO - j + tm, cs]
        cnt = jnp.minimum(t_pos + 1, w).astype(F32)
        diff = win / cnt - h[:, cs]
        y = jnp.dot(diff.astype(BF16), pw_ref[gi], preferred_element_type=F32) + pb_ref[gi:gi + 1, :]
        o_ref[:, cs] = x[:, cs] + gate[:, cs] * (y * ps_ref[:, cs])


def _pool_mixer(x, g, mod, p_w, p_b, p_scale):
    s, d = x.shape
    ng, gd, _ = p_w.shape
    return pl.pallas_call(
        _pool_kernel,
        grid=(s // ROW_TILE,),
        in_specs=[
            _row_spec(ROW_TILE, d),
            _resident((1, d)),
            _resident((3, d)),
            _resident((ng, gd, gd)),
            _resident((ng, gd)),
            _resident((1, d)),
        ],
        out_specs=_row_spec(ROW_TILE, d),
        out_shape=jax.ShapeDtypeStruct((s, d), F32),
        scratch_shapes=[pltpu.VMEM((POOL_HALO + ROW_TILE, d), F32)],
        compiler_params=_params(("arbitrary",)),
        name="pool_mixer",
    )(x, g.reshape(1, d), mod, p_w.astype(BF16), p_b, p_scale.reshape(1, d))


def kernel(x, c, cond_w, cond_b, ada_w, ada_b, norm_g, ffn_w1, ffn_w3, ffn_w2, a_w_in, a_b_in, a_dw, a_dw_b, a_ln_g, a_ln_b, a_w_out, a_b_out, b_w_qkv, b_q_g, b_k_g, b_w_o, p_w, p_b, p_scale):
    batch, s, d = x.shape
    assert batch == 1 and d == N_HEADS * HEAD_DIM
    assert s % ROW_TILE == 0 and s % ATT_TILE == 0 and s // ATT_TILE >= ATT_FAST_BLOCKS + ATT_UNROLL
    depth = ada_w.shape[0]
    mod = _conditioning(c, cond_w, cond_b, ada_w, ada_b)
    xs = x.reshape(s, d)
    ia = ib = ic = 0
    for i in range(depth):
        xs = _ffn(xs, norm_g[i, 0], mod[i, 0], ffn_w1[i, 0], ffn_w3[i, 0], ffn_w2[i, 0])
        kind = i % 3
        if kind == 0:
            xs = _conv_mixer(xs, norm_g[i, 1], mod[i, 1], a_w_in[ia], a_b_in[ia], a_dw[ia], a_dw_b[ia],
                             a_ln_g[ia], a_ln_b[ia], a_w_out[ia], a_b_out[ia])
            ia += 1
        elif kind == 1:
            qkv = _qkv_proj(xs, norm_g[i, 1], mod[i, 1], b_w_qkv[ib], b_q_g[ib], b_k_g[ib])
            attn = _attention(qkv, d)
            xs = _out_proj(xs, attn, mod[i, 1], b_w_o[ib])
            ib += 1
        else:
            xs = _pool_mixer(xs, norm_g[i, 1], mod[i, 1], p_w[ic], p_b[ic], p_scale[ic])
            ic += 1
        xs = _ffn(xs, norm_g[i, 2], mod[i, 2], ffn_w1[i, 1], ffn_w3[i, 1], ffn_w2[i, 1])
    return xs.reshape(batch, s, d)
```

```python
import jax
import jax.numpy as jnp
from jax import lax
from jax.experimental import pallas as pl
from jax.experimental.pallas import tpu as pltpu

F32 = jnp.float32
BF16 = jnp.bfloat16

EPS = 1e-6
N_HEADS = 16
HEAD_DIM = 64
CONV_WIDTH = 31
POOL_WINDOWS = (2, 4, 8, 16)
LANES = 128
SUBLANES = 8
HEADS_PER_BLOCK = LANES // HEAD_DIM

ROW_TILE = 512
FFN_TILE = 1024
FF_CHUNK = 256
CONV_HALO = 32
CONV_SUB = 128
POOL_HALO = 16
ATT_TILE = 128
ATT_FAST_BLOCKS = 2
ATT_UNROLL = 6
ATT_ZERO_WEIGHT = 104.0
VMEM_LIMIT = 56 * 1024 * 1024


def _sigmoid(x):
    return 1.0 / (1.0 + jnp.exp(-x))


def _modulate(x, g, mod_ref):
    ms = jnp.mean(x * x, axis=-1, keepdims=True)
    gs = g * (1.0 + mod_ref[1:2, :])
    return (x * lax.rsqrt(ms + EPS)) * gs + mod_ref[0:1, :]


def _resident(shape):
    zeros = (0,) * len(shape)
    return pl.BlockSpec(shape, lambda *_: zeros, pipeline_mode=pl.Buffered(1))


def _row_spec(tile, width):
    return pl.BlockSpec((tile, width), lambda i: (i, 0))


def _params(semantics):
    return pltpu.CompilerParams(dimension_semantics=semantics, vmem_limit_bytes=VMEM_LIMIT)


def _cond_embed_kernel(c_ref, w_ref, b_ref, e_ref):
    t = jnp.sum(c_ref[...] * w_ref[...], axis=0, keepdims=True) + b_ref[...]
    e_ref[...] = t * _sigmoid(t)


def _cond_mod_kernel(e_ref, w_ref, b_ref, o_ref):
    o_ref[0] = jnp.sum(e_ref[...] * w_ref[0], axis=0, keepdims=True) + b_ref[0]


def _conditioning(c, cond_w, cond_b, ada_w, ada_b):
    d = cond_w.shape[0]
    depth, _, n = ada_w.shape
    e = pl.pallas_call(
        _cond_embed_kernel,
        out_shape=jax.ShapeDtypeStruct((1, d), F32),
        name="cond_embed",
    )(c.reshape(d, 1), cond_w, cond_b.reshape(1, d))
    tn = n // 8
    mod = pl.pallas_call(
        _cond_mod_kernel,
        grid=(depth, n // tn),
        in_specs=[
            pl.BlockSpec((d, 1), lambda l, j: (0, 0)),
            pl.BlockSpec((1, d, tn), lambda l, j: (l, 0, j)),
            pl.BlockSpec((1, 1, tn), lambda l, j: (l, 0, j)),
        ],
        out_specs=pl.BlockSpec((1, 1, tn), lambda l, j: (l, 0, j)),
        out_shape=jax.ShapeDtypeStruct((depth, 1, n), F32),
        compiler_params=_params(("arbitrary", "arbitrary")),
        name="cond_mod",
    )(e.reshape(d, 1), ada_w, ada_b.reshape(depth, 1, n))
    return mod.reshape(depth, 3, 3, d)


def _ffn_kernel(x_ref, g_ref, mod_ref, w1_ref, w3_ref, w2_ref, o_ref, h_ref, acc_ref):
    x = x_ref[...]
    h_ref[...] = _modulate(x, g_ref[...], mod_ref).astype(BF16)
    acc_ref[...] = jnp.zeros_like(acc_ref)

    for f in range(w1_ref.shape[1] // FF_CHUNK):
        fs = slice(f * FF_CHUNK, (f + 1) * FF_CHUNK)
        h = h_ref[...]
        a = jnp.dot(h, w1_ref[:, fs], preferred_element_type=F32)
        b = jnp.dot(h, w3_ref[:, fs], preferred_element_type=F32)
        act = (a * _sigmoid(a) * b).astype(BF16)
        acc_ref[...] += jnp.dot(act, w2_ref[fs, :], preferred_element_type=F32)
    o_ref[...] = x + (0.5 * (1.0 + mod_ref[2:3, :])) * acc_ref[...]


def _stacked(stack, index):
    lead = len(index)
    block = (None,) * lead + tuple(stack.shape[lead:])
    where = tuple(index) + (0,) * (stack.ndim - lead)
    return pl.BlockSpec(block, lambda *_: where, pipeline_mode=pl.Buffered(1))


def _ffn(x, g, mod, w1, w3, w2, index):
    s, d = x.shape
    f = w1.shape[-1]
    assert f % FF_CHUNK == 0 and s % FFN_TILE == 0
    return pl.pallas_call(
        _ffn_kernel,
        grid=(s // FFN_TILE,),
        in_specs=[
            _row_spec(FFN_TILE, d),
            _resident((1, d)),
            _resident((3, d)),
            _stacked(w1, index),
            _stacked(w3, index),
            _stacked(w2, index),
        ],
        out_specs=_row_spec(FFN_TILE, d),
        out_shape=jax.ShapeDtypeStruct((s, d), F32),
        scratch_shapes=[pltpu.VMEM((FFN_TILE, d), BF16), pltpu.VMEM((FFN_TILE, d), F32)],
        compiler_params=_params(("arbitrary",)),
        name="ffn",
    )(x, g.reshape(1, d), mod, w1, w3, w2)


def _conv_kernel(x_ref, g_ref, mod_ref, win_ref, bin_ref, dw_ref, dwb_ref, lng_ref, lnb_ref,
                 wout_ref, bout_ref, o_ref, u_ref, us_ref, v_ref):
    tm, d = x_ref.shape
    i = pl.program_id(0)

    @pl.when(i == 0)
    def _():
        u_ref[0:CONV_HALO, :] = jnp.zeros((CONV_HALO, d), F32)

    @pl.when(i > 0)
    def _():
        u_ref[0:CONV_HALO, :] = u_ref[tm:tm + CONV_HALO, :]

    gate = 1.0 + mod_ref[2:3, :]
    base = CONV_HALO - (CONV_WIDTH - 1)

    def glu_in(k):
        rs = slice(k * CONV_SUB, (k + 1) * CONV_SUB)
        h = _modulate(x_ref[rs, :], g_ref[...], mod_ref).astype(BF16)
        uv = jnp.dot(h, win_ref[...], preferred_element_type=F32) + bin_ref[...]
        u_ref[CONV_HALO + k * CONV_SUB:CONV_HALO + (k + 1) * CONV_SUB, :] = uv[:, :d] * _sigmoid(uv[:, d:])

    def depthwise(k):
        lo = k * CONV_SUB
        for b in range(1, SUBLANES):
            us_ref[b - 1] = u_ref[lo + b:lo + b + us_ref.shape[1], :]
        for c in range(d // LANES):
            cs = slice(c * LANES, (c + 1) * LANES)
            taps = [jnp.broadcast_to(dw_ref[j:j + 1, cs], (SUBLANES, LANES)) for j in range(CONV_WIDTH)]
            bias = jnp.broadcast_to(dwb_ref[:, cs], (SUBLANES, LANES))
            for r in range(0, CONV_SUB, SUBLANES):
                acc = bias
                for j in range(CONV_WIDTH):
                    a, b = divmod(base + j, SUBLANES)
                    r0 = r + a * SUBLANES
                    src = u_ref[lo + r0:lo + r0 + SUBLANES, cs] if b == 0 else us_ref[b - 1, r0:r0 + SUBLANES, cs]
                    acc = acc + taps[j] * src
                v_ref[lo + r:lo + r + SUBLANES, cs] = acc

    def norm_out(k):
        rs = slice(k * CONV_SUB, (k + 1) * CONV_SUB)
        v = v_ref[rs, :]
        mu = jnp.mean(v, axis=-1, keepdims=True)
        vc = v - mu
        var = jnp.mean(vc * vc, axis=-1, keepdims=True)
        y = vc * lax.rsqrt(var + EPS) * lng_ref[...] + lnb_ref[...]
        y = (y * _sigmoid(y)).astype(BF16)
        y = jnp.dot(y, wout_ref[...], preferred_element_type=F32) + bout_ref[...]
        o_ref[rs, :] = x_ref[rs, :] + gate * y

    n_sub = tm // CONV_SUB
    glu_in(0)
    for k in range(n_sub):
        if k + 1 < n_sub:
            glu_in(k + 1)
        depthwise(k)
        norm_out(k)


def _conv_mixer(x, g, mod, w_in, b_in, dw, dw_b, ln_g, ln_b, w_out, b_out):
    s, d = x.shape
    row = lambda v: v.reshape(1, -1)
    return pl.pallas_call(
        _conv_kernel,
        grid=(s // ROW_TILE,),
        in_specs=[
            _row_spec(ROW_TILE, d),
            _resident((1, d)),
            _resident((3, d)),
            _resident((d, 2 * d)),
            _resident((1, 2 * d)),
            _resident((CONV_WIDTH, d)),
            _resident((1, d)),
            _resident((1, d)),
            _resident((1, d)),
            _resident((d, d)),
            _resident((1, d)),
        ],
        out_specs=_row_spec(ROW_TILE, d),
        out_shape=jax.ShapeDtypeStruct((s, d), F32),
        scratch_shapes=[
            pltpu.VMEM((CONV_HALO + ROW_TILE, d), F32),
            pltpu.VMEM((SUBLANES - 1, CONV_SUB + CONV_HALO - SUBLANES, d), F32),
            pltpu.VMEM((ROW_TILE, d), F32),
        ],
        compiler_params=_params(("arbitrary",)),
        name="conv_mixer",
    )(x, row(g), mod, w_in.astype(BF16), row(b_in), dw, row(dw_b), row(ln_g), row(ln_b),
      w_out.astype(BF16), row(b_out))


def _qkv_kernel(x_ref, g_ref, mod_ref, w_ref, qg_ref, kg_ref, o_ref):
    tm, d = x_ref.shape
    h = _modulate(x_ref[...], g_ref[...], mod_ref).astype(BF16)
    qkv = jnp.dot(h, w_ref[...], preferred_element_type=F32)
    first = lax.broadcasted_iota(jnp.int32, (1, LANES), 1) < HEAD_DIM

    def head_norm(blk, gain):
        sq = blk * blk
        ms_a = jnp.sum(jnp.where(first, sq, 0.0), axis=-1, keepdims=True) * (1.0 / HEAD_DIM)
        ms_b = jnp.sum(jnp.where(first, 0.0, sq), axis=-1, keepdims=True) * (1.0 / HEAD_DIM)
        r = jnp.where(first, lax.rsqrt(ms_a + EPS), lax.rsqrt(ms_b + EPS))
        return blk * r * gain

    nb = d // LANES
    for j in range(nb):
        cs = slice(j * LANES, (j + 1) * LANES)
        o_ref[:, cs] = (head_norm(qkv[:, cs], qg_ref[...]) * (HEAD_DIM ** -0.5)).astype(BF16)
    for j in range(nb, 2 * nb):
        cs = slice(j * LANES, (j + 1) * LANES)
        o_ref[:, cs] = head_norm(qkv[:, cs], kg_ref[...]).astype(BF16)
    o_ref[:, 2 * d:] = qkv[:, 2 * d:].astype(BF16)


def _qkv_proj(x, g, mod, w_qkv, q_g, k_g):
    s, d = x.shape
    tile_gain = lambda v: jnp.tile(v, HEADS_PER_BLOCK).reshape(1, LANES)
    return pl.pallas_call(
        _qkv_kernel,
        grid=(s // ROW_TILE,),
        in_specs=[
            _row_spec(ROW_TILE, d),
            _resident((1, d)),
            _resident((3, d)),
            _resident((d, 3 * d)),
            _resident((1, LANES)),
            _resident((1, LANES)),
        ],
        out_specs=_row_spec(ROW_TILE, 3 * d),
        out_shape=jax.ShapeDtypeStruct((s, 3 * d), BF16),
        compiler_params=_params(("arbitrary",)),
        name="qkv_proj",
    )(x, g.reshape(1, d), mod, w_qkv.astype(BF16), tile_gain(q_g), tile_gain(k_g))


def _attn_kernel(q_ref, k_ref, v_ref, o_ref, c_ref, acc_ref):
    n = ATT_TILE
    s = q_ref.shape[0]
    first = lax.broadcasted_iota(jnp.int32, (1, LANES), 1) < HEAD_DIM
    row = lax.broadcasted_iota(jnp.int32, (n, n), 0)
    col = lax.broadcasted_iota(jnp.int32, (n, n), 1)
    later_and_total = jnp.concatenate([(row > col).astype(BF16), jnp.ones((n, n), BF16)], axis=1)
    strictly_causal = jnp.concatenate([col < row, col < row], axis=1)
    zero = jnp.zeros((), BF16)

    def key_blocks(jobs, c, acc):
        k2, v2 = {}, {}
        for _, _, kb, _ in jobs:
            if id(kb) not in k2:
                k0 = pl.multiple_of(kb * n, n)
                k = k_ref[pl.ds(k0, n), :]
                v = v_ref[pl.ds(k0, n), :]
                k2[id(kb)] = jnp.concatenate([jnp.where(first, k, zero), jnp.where(first, zero, k)], axis=0)
                v2[id(kb)] = jnp.concatenate([jnp.where(first, v, zero), jnp.where(first, zero, v)], axis=0)
        z = [lax.dot_general(q, k2[id(kb)], (((1,), (1,)), ((), ())), preferred_element_type=F32)
             for _, q, kb, _ in jobs]
        sp = [jnp.maximum(zi, 0.0) + jnp.log(1.0 + jnp.exp(-jnp.abs(zi))) for zi in z]
        spb = [(jnp.where(strictly_causal, si, 0.0) if job[3] else si).astype(BF16) for si, job in zip(sp, jobs)]
        run_a = [jnp.dot(si[:, :n], later_and_total, preferred_element_type=F32) for si in spb]
        run_b = [jnp.dot(si[:, n:], later_and_total, preferred_element_type=F32) for si in spb]
        a = []
        for i, (u, _, _, diagonal) in enumerate(jobs):
            suffix = jnp.concatenate([run_a[i][:, :n], run_b[i][:, :n]], axis=1) + c[u]
            c[u] = c[u] + jnp.concatenate([run_a[i][:, n:], run_b[i][:, n:]], axis=1)
            ai = jnp.exp(z[i] - sp[i] - suffix)
            if diagonal:
                ai = jnp.where(strictly_causal, ai, 0.0)
            a.append(ai.astype(BF16))
        for i, (u, _, kb, _) in enumerate(jobs):
            acc[u] = acc[u] + jnp.dot(a[i], v2[id(kb)], preferred_element_type=F32)

    def key_block(q, kb, diagonal, c, acc):
        c, acc = [c], [acc]
        key_blocks([(0, q, kb, diagonal)], c, acc)
        return c[0], acc[0]

    def query_tiles(base, n_tiles, n_fast):
        blocks = {off: base + off for off in range(-n_fast, n_tiles)}
        jobs = []
        for u in range(n_tiles):
            q = q_ref[pl.ds(pl.multiple_of(blocks[u] * n, n), n), :]
            jobs += [(u, q, blocks[u - j], j == 0) for j in range(n_fast + 1)]
        c = [jnp.zeros((n, 2 * n), F32)] * n_tiles
        acc = [jnp.zeros((n, LANES), F32)] * n_tiles
        key_blocks(jobs, c, acc)
        c_all = None
        for u in range(n_tiles):
            c_ref[u] = c[u]
            acc_ref[u] = acc[u]
            c_all = c[u] if c_all is None else jnp.minimum(c_all, c[u])

        @pl.when(jnp.min(c_all) < ATT_ZERO_WEIGHT)
        def _():
            for u in range(n_tiles):
                qb = base + u
                q = q_ref[pl.ds(pl.multiple_of(qb * n, n), n), :]

                def more(carry):
                    kb, c_min = carry
                    return jnp.logical_and(kb >= 0, c_min < ATT_ZERO_WEIGHT)

                def step(carry, u=u, q=q):
                    kb, _ = carry
                    c, acc = key_block(q, kb, False, c_ref[u], acc_ref[u])
                    c_ref[u] = c
                    acc_ref[u] = acc
                    return kb - 1, jnp.min(c)

                lax.while_loop(more, step, (qb - n_fast - 1, jnp.min(c_ref[u])))

        for u in range(n_tiles):
            q0 = pl.multiple_of((base + u) * n, n)
            o_ref[pl.ds(q0, n), :] = acc_ref[u].astype(o_ref.dtype)

    def head_tile(qb, carry):
        query_tiles(qb, 1, 0)
        return carry

    def body_tiles(g, carry):
        query_tiles(n_head + g * ATT_UNROLL, ATT_UNROLL, ATT_FAST_BLOCKS)
        return carry

    n_total = s // n
    n_head = ATT_FAST_BLOCKS + (n_total - ATT_FAST_BLOCKS) % ATT_UNROLL
    lax.fori_loop(0, n_head, head_tile, 0)
    lax.fori_loop(0, (n_total - n_head) // ATT_UNROLL, body_tiles, 0)


def _attention(qkv, d):
    s = qkv.shape[0]
    nb = d // LANES
    return pl.pallas_call(
        _attn_kernel,
        grid=(nb,),
        in_specs=[
            pl.BlockSpec((s, LANES), lambda hb: (0, hb)),
            pl.BlockSpec((s, LANES), lambda hb: (0, nb + hb)),
            pl.BlockSpec((s, LANES), lambda hb: (0, 2 * nb + hb)),
        ],
        out_specs=pl.BlockSpec((s, LANES), lambda hb: (0, hb)),
        out_shape=jax.ShapeDtypeStruct((s, d), BF16),
        scratch_shapes=[pltpu.VMEM((ATT_UNROLL, ATT_TILE, 2 * ATT_TILE), F32),
                        pltpu.VMEM((ATT_UNROLL, ATT_TILE, LANES), F32)],
        compiler_params=_params(("arbitrary",)),
        name="stickbreak_attn",
    )(qkv, qkv, qkv)


def _oproj_kernel(x_ref, a_ref, mod_ref, w_ref, o_ref):
    y = jnp.dot(a_ref[...], w_ref[...], preferred_element_type=F32)
    o_ref[...] = x_ref[...] + (1.0 + mod_ref[2:3, :]) * y


def _out_proj(x, attn, mod, w_o):
    s, d = x.shape
    return pl.pallas_call(
        _oproj_kernel,
        grid=(s // ROW_TILE,),
        in_specs=[_row_spec(ROW_TILE, d), _row_spec(ROW_TILE, d), _resident((3, d)), _resident((d, d))],
        out_specs=_row_spec(ROW_TILE, d),
        out_shape=jax.ShapeDtypeStruct((s, d), F32),
        compiler_params=_params(("arbitrary",)),
        name="attn_out_proj",
    )(x, attn, mod, w_o.astype(BF16))


def _pool_kernel(x_ref, g_ref, mod_ref, pw_ref, pb_ref, ps_ref, o_ref, h_ref):
    tm, d = x_ref.shape
    gd = d // len(POOL_WINDOWS)
    i = pl.program_id(0)

    @pl.when(i == 0)
    def _():
        h_ref[0:POOL_HALO, :] = jnp.zeros((POOL_HALO, d), F32)

    @pl.when(i > 0)
    def _():
        h_ref[0:POOL_HALO, :] = h_ref[tm:tm + POOL_HALO, :]

    x = x_ref[...]
    h = _modulate(x, g_ref[...], mod_ref)
    h_ref[POOL_HALO:POOL_HALO + tm, :] = h
    t_pos = i * tm + lax.broadcasted_iota(jnp.int32, (tm, 1), 0)
    gate = 1.0 + mod_ref[2:3, :]
    for gi, w in enumerate(POOL_WINDOWS):
        cs = slice(gi * gd, (gi + 1) * gd)
        win = h[:, cs]
        for j in range(1, w):
            win = win + h_ref[POOL_HALO - j:POOL_HALO - j + tm, cs]
        cnt = jnp.minimum(t_pos + 1, w).astype(F32)
        diff = win / cnt - h[:, cs]
        y = jnp.dot(diff.astype(BF16), pw_ref[gi], preferred_element_type=F32) + pb_ref[gi:gi + 1, :]
        o_ref[:, cs] = x[:, cs] + gate[:, cs] * (y * ps_ref[:, cs])


def _pool_mixer(x, g, mod, p_w, p_b, p_scale):
    s, d = x.shape
    ng, gd, _ = p_w.shape
    return pl.pallas_call(
        _pool_kernel,
        grid=(s // ROW_TILE,),
        in_specs=[
            _row_spec(ROW_TILE, d),
            _resident((1, d)),
            _resident((3, d)),
            _resident((ng, gd, gd)),
            _resident((ng, gd)),
            _resident((1, d)),
        ],
        out_specs=_row_spec(ROW_TILE, d),
        out_shape=jax.ShapeDtypeStruct((s, d), F32),
        scratch_shapes=[pltpu.VMEM((POOL_HALO + ROW_TILE, d), F32)],
        compiler_params=_params(("arbitrary",)),
        name="pool_mixer",
    )(x, g.reshape(1, d), mod, p_w.astype(BF16), p_b, p_scale.reshape(1, d))


def kernel(x, c, cond_w, cond_b, ada_w, ada_b, norm_g, ffn_w1, ffn_w3, ffn_w2, a_w_in, a_b_in, a_dw, a_dw_b, a_ln_g, a_ln_b, a_w_out, a_b_out, b_w_qkv, b_q_g, b_k_g, b_w_o, p_w, p_b, p_scale):
    batch, s, d = x.shape
    assert batch == 1 and d == N_HEADS * HEAD_DIM
    assert s % ROW_TILE == 0 and s % ATT_TILE == 0 and s // ATT_TILE >= ATT_FAST_BLOCKS + ATT_UNROLL
    depth = ada_w.shape[0]
    mod = _conditioning(c, cond_w, cond_b, ada_w, ada_b)
    xs = x.reshape(s, d)
    w1, w3, w2 = ffn_w1.astype(BF16), ffn_w3.astype(BF16), ffn_w2.astype(BF16)
    ia = ib = ic = 0
    for i in range(depth):
        xs = _ffn(xs, norm_g[i, 0], mod[i, 0], w1, w3, w2, (i, 0))
        kind = i % 3
        if kind == 0:
            xs = _conv_mixer(xs, norm_g[i, 1], mod[i, 1], a_w_in[ia], a_b_in[ia], a_dw[ia], a_dw_b[ia],
                             a_ln_g[ia], a_ln_b[ia], a_w_out[ia], a_b_out[ia])
            ia += 1
        elif kind == 1:
            qkv = _qkv_proj(xs, norm_g[i, 1], mod[i, 1], b_w_qkv[ib], b_q_g[ib], b_k_g[ib])
            attn = _attention(qkv, d)
            xs = _out_proj(xs, attn, mod[i, 1], b_w_o[ib])
            ib += 1
        else:
            xs = _pool_mixer(xs, norm_g[i, 1], mod[i, 1], p_w[ic], p_b[ic], p_scale[ic])
            ic += 1
        xs = _ffn(xs, norm_g[i, 2], mod[i, 2], w1, w3, w2, (i, 1))
    return xs.reshape(batch, s, d)
```

```python
import functools

import jax
import jax.numpy as jnp
from jax import lax
from jax.experimental import pallas as pl
from jax.experimental.pallas import tpu as pltpu

F32 = jnp.float32
BF16 = jnp.bfloat16

EPS = 1e-6
N_HEADS = 16
HEAD_DIM = 64
CONV_WIDTH = 31
POOL_WINDOWS = (2, 4, 8, 16)
LANES = 128
SUBLANES = 8
HEADS_PER_BLOCK = LANES // HEAD_DIM

ROW_TILE = 512
FFN_TILE = 1024
FF_CHUNK = 256
CONV_HALO = 32
CONV_SUB = 128
POOL_HALO = 16
ATT_TILE = 128
ATT_FAST_BLOCKS = 2
ATT_UNROLL = 6
ATT_ZERO_WEIGHT = 104.0
VMEM_LIMIT = 56 * 1024 * 1024


def _sigmoid(x):
    return 1.0 / (1.0 + jnp.exp(-x))


def _modulate(x, g, mod_ref):
    ms = jnp.mean(x * x, axis=-1, keepdims=True)
    gs = g * (1.0 + mod_ref[1:2, :])
    return (x * lax.rsqrt(ms + EPS)) * gs + mod_ref[0:1, :]


def _resident(shape):
    zeros = (0,) * len(shape)
    return pl.BlockSpec(shape, lambda *_: zeros, pipeline_mode=pl.Buffered(1))


def _row_spec(tile, width):
    return pl.BlockSpec((tile, width), lambda i: (i, 0))


def _params(semantics):
    return pltpu.CompilerParams(dimension_semantics=semantics, vmem_limit_bytes=VMEM_LIMIT)


def _cond_embed_kernel(c_ref, w_ref, b_ref, e_ref):
    t = jnp.sum(c_ref[...] * w_ref[...], axis=0, keepdims=True) + b_ref[...]
    e_ref[...] = t * _sigmoid(t)


def _cond_mod_kernel(e_ref, w_ref, b_ref, o_ref):
    o_ref[0] = jnp.sum(e_ref[...] * w_ref[0], axis=0, keepdims=True) + b_ref[0]


def _conditioning(c, cond_w, cond_b, ada_w, ada_b):
    d = cond_w.shape[0]
    depth, _, n = ada_w.shape
    e = pl.pallas_call(
        _cond_embed_kernel,
        out_shape=jax.ShapeDtypeStruct((1, d), F32),
        name="cond_embed",
    )(c.reshape(d, 1), cond_w, cond_b.reshape(1, d))
    tn = n // 8
    mod = pl.pallas_call(
        _cond_mod_kernel,
        grid=(depth, n // tn),
        in_specs=[
            pl.BlockSpec((d, 1), lambda l, j: (0, 0)),
            pl.BlockSpec((1, d, tn), lambda l, j: (l, 0, j)),
            pl.BlockSpec((1, 1, tn), lambda l, j: (l, 0, j)),
        ],
        out_specs=pl.BlockSpec((1, 1, tn), lambda l, j: (l, 0, j)),
        out_shape=jax.ShapeDtypeStruct((depth, 1, n), F32),
        compiler_params=_params(("arbitrary", "arbitrary")),
        name="cond_mod",
    )(e.reshape(d, 1), ada_w, ada_b.reshape(depth, 1, n))
    return mod.reshape(depth, 3, 3, d)


def _ffn_stages(x_ref, g_ref, mod_ref, w1_ref, w3_ref, w2_ref, o_ref, h_ref, acc_ref):
    def start():
        h_ref[...] = _modulate(x_ref[...], g_ref[...], mod_ref).astype(BF16)
        acc_ref[...] = jnp.zeros_like(acc_ref)

    def chunk(f):
        fs = slice(f * FF_CHUNK, (f + 1) * FF_CHUNK)
        h = h_ref[...]
        a = jnp.dot(h, w1_ref[:, fs], preferred_element_type=F32)
        b = jnp.dot(h, w3_ref[:, fs], preferred_element_type=F32)
        act = (a * _sigmoid(a) * b).astype(BF16)
        acc_ref[...] += jnp.dot(act, w2_ref[fs, :], preferred_element_type=F32)

    def finish():
        o_ref[...] = x_ref[...] + (0.5 * (1.0 + mod_ref[2:3, :])) * acc_ref[...]

    return start, chunk, finish


def _ffn_kernel(*refs, n_cast, has_proj):
    refs = list(refs)
    x_ref, g_ref, mod_ref, w1_ref, w3_ref, w2_ref = refs[:6]
    del refs[:6]
    if has_proj:
        attn_ref, pmod_ref, wo_ref = refs[:3]
        del refs[:3]
    cast_in = refs[:n_cast]
    o_ref = refs[n_cast]
    cast_out = refs[n_cast + 1:2 * n_cast + 1]
    h_ref, acc_ref = refs[2 * n_cast + 1:2 * n_cast + 3]

    for src_ref, dst_ref in zip(cast_in, cast_out):
        dst_ref[...] = src_ref[...].astype(BF16)

    if has_proj:
        xin_ref = refs[-1]
        y = jnp.dot(attn_ref[...], wo_ref[...], preferred_element_type=F32)
        xin_ref[...] = x_ref[...] + (1.0 + pmod_ref[2:3, :]) * y
    else:
        xin_ref = x_ref

    start, chunk, finish = _ffn_stages(xin_ref, g_ref, mod_ref, w1_ref, w3_ref, w2_ref, o_ref, h_ref, acc_ref)
    start()
    for f in range(w1_ref.shape[1] // FF_CHUNK):
        chunk(f)
    finish()


def _ffn(x, g, mod, weights, next_weights=None, proj=None):
    s, d = x.shape
    w1, w3, w2 = weights
    f = w1.shape[-1]
    steps = s // FFN_TILE
    assert f % FF_CHUNK == 0 and s % FFN_TILE == 0
    in_specs = [_row_spec(FFN_TILE, d), _resident((1, d)), _resident((3, d)),
                _resident(w1.shape), _resident(w3.shape), _resident(w2.shape)]
    args = [x, g.reshape(1, d), mod, w1, w3, w2]
    scratch = [pltpu.VMEM((FFN_TILE, d), BF16), pltpu.VMEM((FFN_TILE, d), F32)]
    if proj is not None:
        attn, pmod, w_o = proj
        in_specs += [_row_spec(FFN_TILE, d), _resident((3, d)), _resident((d, d))]
        args += [attn, pmod, w_o]
        scratch.append(pltpu.VMEM((FFN_TILE, d), F32))
    out_specs = [_row_spec(FFN_TILE, d)]
    out_shape = [jax.ShapeDtypeStruct((s, d), F32)]
    for stack, index in next_weights or ():
        rows, cols = stack.shape[-2:]
        slab = rows // steps
        assert rows % steps == 0 and slab % (2 * SUBLANES) == 0
        lead = tuple(index)
        in_specs.append(pl.BlockSpec((None,) * len(lead) + (slab, cols), lambda i, lead=lead: lead + (i, 0)))
        args.append(stack)
        out_specs.append(pl.BlockSpec((slab, cols), lambda i: (i, 0)))
        out_shape.append(jax.ShapeDtypeStruct((rows, cols), BF16))
    n_cast = len(out_specs) - 1
    out = pl.pallas_call(
        functools.partial(_ffn_kernel, n_cast=n_cast, has_proj=proj is not None),
        grid=(steps,),
        in_specs=in_specs,
        out_specs=out_specs,
        out_shape=out_shape,
        scratch_shapes=scratch,
        compiler_params=_params(("arbitrary",)),
        name="ffn",
    )(*args)
    return out[0], tuple(out[1:])


def _conv_stages(x_ref, g_ref, mod_ref, win_ref, bin_ref, dw_ref, dwb_ref, lng_ref, lnb_ref,
                 wout_ref, bout_ref, o_ref, u_ref, us_ref, v_ref):
    d = x_ref.shape[1]
    gate = 1.0 + mod_ref[2:3, :]
    base = CONV_HALO - (CONV_WIDTH - 1)

    def glu_in(k):
        rs = slice(k * CONV_SUB, (k + 1) * CONV_SUB)
        h = _modulate(x_ref[rs, :], g_ref[...], mod_ref).astype(BF16)
        uv = jnp.dot(h, win_ref[...], preferred_element_type=F32) + bin_ref[...]
        u_ref[CONV_HALO + k * CONV_SUB:CONV_HALO + (k + 1) * CONV_SUB, :] = uv[:, :d] * _sigmoid(uv[:, d:])

    def shifted(k):
        lo = k * CONV_SUB
        for b in range(1, SUBLANES):
            us_ref[b - 1] = u_ref[lo + b:lo + b + us_ref.shape[1], :]

    def depthwise(k, c):
        lo = k * CONV_SUB
        cs = slice(c * LANES, (c + 1) * LANES)
        taps = [jnp.broadcast_to(dw_ref[j:j + 1, cs], (SUBLANES, LANES)) for j in range(CONV_WIDTH)]
        bias = jnp.broadcast_to(dwb_ref[:, cs], (SUBLANES, LANES))
        for r in range(0, CONV_SUB, SUBLANES):
            acc = bias
            for j in range(CONV_WIDTH):
                a, b = divmod(base + j, SUBLANES)
                r0 = r + a * SUBLANES
                src = u_ref[lo + r0:lo + r0 + SUBLANES, cs] if b == 0 else us_ref[b - 1, r0:r0 + SUBLANES, cs]
                acc = acc + taps[j] * src
            v_ref[r:r + SUBLANES, cs] = acc

    def norm_out(k):
        rs = slice(k * CONV_SUB, (k + 1) * CONV_SUB)
        v = v_ref[...]
        mu = jnp.mean(v, axis=-1, keepdims=True)
        vc = v - mu
        var = jnp.mean(vc * vc, axis=-1, keepdims=True)
        y = vc * lax.rsqrt(var + EPS) * lng_ref[...] + lnb_ref[...]
        y = (y * _sigmoid(y)).astype(BF16)
        y = jnp.dot(y, wout_ref[...], preferred_element_type=F32) + bout_ref[...]
        o_ref[rs, :] = x_ref[rs, :] + gate * y

    return glu_in, shifted, depthwise, norm_out


def _conv_kernel(x_ref, g_ref, mod_ref, win_ref, bin_ref, dw_ref, dwb_ref, lng_ref, lnb_ref,
                 wout_ref, bout_ref, o_ref, u_ref, us_ref, v_ref):
    tm, d = x_ref.shape
    i = pl.program_id(0)

    @pl.when(i == 0)
    def _():
        u_ref[0:CONV_HALO, :] = jnp.zeros((CONV_HALO, d), F32)

    @pl.when(i > 0)
    def _():
        u_ref[0:CONV_HALO, :] = u_ref[tm:tm + CONV_HALO, :]

    glu_in, shifted, depthwise, norm_out = _conv_stages(
        x_ref, g_ref, mod_ref, win_ref, bin_ref, dw_ref, dwb_ref, lng_ref, lnb_ref, wout_ref, bout_ref,
        o_ref, u_ref, us_ref, v_ref)

    n_sub = tm // CONV_SUB
    glu_in(0)
    for k in range(n_sub):
        if k + 1 < n_sub:
            glu_in(k + 1)
        shifted(k)
        for c in range(d // LANES):
            depthwise(k, c)
        norm_out(k)


def _conv_mixer(x, g, mod, w_in, b_in, dw, dw_b, ln_g, ln_b, w_out, b_out):
    s, d = x.shape
    assert s % ROW_TILE == 0 and ROW_TILE % CONV_SUB == 0
    row = lambda v: v.reshape(1, -1)
    return pl.pallas_call(
        _conv_kernel,
        grid=(s // ROW_TILE,),
        in_specs=[
            _row_spec(ROW_TILE, d),
            _resident((1, d)),
            _resident((3, d)),
            _resident((d, 2 * d)),
            _resident((1, 2 * d)),
            _resident((CONV_WIDTH, d)),
            _resident((1, d)),
            _resident((1, d)),
            _resident((1, d)),
            _resident((d, d)),
            _resident((1, d)),
        ],
        out_specs=_row_spec(ROW_TILE, d),
        out_shape=jax.ShapeDtypeStruct((s, d), F32),
        scratch_shapes=[
            pltpu.VMEM((CONV_HALO + ROW_TILE, d), F32),
            pltpu.VMEM((SUBLANES - 1, CONV_SUB + CONV_HALO - SUBLANES, d), F32),
            pltpu.VMEM((CONV_SUB, d), F32),
        ],
        compiler_params=_params(("arbitrary",)),
        name="conv_mixer",
    )(x, row(g), mod, w_in.astype(BF16), row(b_in), dw, row(dw_b), row(ln_g), row(ln_b),
      w_out.astype(BF16), row(b_out))


def _qkv_kernel(x_ref, g_ref, mod_ref, w_ref, qg_ref, kg_ref, o_ref):
    tm, d = x_ref.shape
    h = _modulate(x_ref[...], g_ref[...], mod_ref).astype(BF16)
    qkv = jnp.dot(h, w_ref[...], preferred_element_type=F32)
    first = lax.broadcasted_iota(jnp.int32, (1, LANES), 1) < HEAD_DIM

    def head_norm(blk, gain):
        sq = blk * blk
        ms_a = jnp.sum(jnp.where(first, sq, 0.0), axis=-1, keepdims=True) * (1.0 / HEAD_DIM)
        ms_b = jnp.sum(jnp.where(first, 0.0, sq), axis=-1, keepdims=True) * (1.0 / HEAD_DIM)
        r = jnp.where(first, lax.rsqrt(ms_a + EPS), lax.rsqrt(ms_b + EPS))
        return blk * r * gain

    nb = d // LANES
    for j in range(nb):
        cs = slice(j * LANES, (j + 1) * LANES)
        o_ref[:, cs] = (head_norm(qkv[:, cs], qg_ref[...]) * (HEAD_DIM ** -0.5)).astype(BF16)
    for j in range(nb, 2 * nb):
        cs = slice(j * LANES, (j + 1) * LANES)
        o_ref[:, cs] = head_norm(qkv[:, cs], kg_ref[...]).astype(BF16)
    o_ref[:, 2 * d:] = qkv[:, 2 * d:].astype(BF16)


def _qkv_proj(x, g, mod, w_qkv, q_g, k_g):
    s, d = x.shape
    tile_gain = lambda v: jnp.tile(v, HEADS_PER_BLOCK).reshape(1, LANES)
    return pl.pallas_call(
        _qkv_kernel,
        grid=(s // ROW_TILE,),
        in_specs=[
            _row_spec(ROW_TILE, d),
            _resident((1, d)),
            _resident((3, d)),
            _resident((d, 3 * d)),
            _resident((1, LANES)),
            _resident((1, LANES)),
        ],
        out_specs=_row_spec(ROW_TILE, 3 * d),
        out_shape=jax.ShapeDtypeStruct((s, 3 * d), BF16),
        compiler_params=_params(("arbitrary",)),
        name="qkv_proj",
    )(x, g.reshape(1, d), mod, w_qkv.astype(BF16), tile_gain(q_g), tile_gain(k_g))


def _attn_kernel(q_ref, k_ref, v_ref, o_ref, c_ref, acc_ref):
    n = ATT_TILE
    s = q_ref.shape[0]
    first = lax.broadcasted_iota(jnp.int32, (1, LANES), 1) < HEAD_DIM
    row = lax.broadcasted_iota(jnp.int32, (n, n), 0)
    col = lax.broadcasted_iota(jnp.int32, (n, n), 1)
    later_and_total = jnp.concatenate([(row > col).astype(BF16), jnp.ones((n, n), BF16)], axis=1)
    strictly_causal = jnp.concatenate([col < row, col < row], axis=1)
    zero = jnp.zeros((), BF16)

    def key_blocks(jobs, c, acc):
        k2, v2 = {}, {}
        for _, _, kb, _ in jobs:
            if id(kb) not in k2:
                k0 = pl.multiple_of(kb * n, n)
                k = k_ref[pl.ds(k0, n), :]
                v = v_ref[pl.ds(k0, n), :]
                k2[id(kb)] = jnp.concatenate([jnp.where(first, k, zero), jnp.where(first, zero, k)], axis=0)
                v2[id(kb)] = jnp.concatenate([jnp.where(first, v, zero), jnp.where(first, zero, v)], axis=0)
        z = [lax.dot_general(q, k2[id(kb)], (((1,), (1,)), ((), ())), preferred_element_type=F32)
             for _, q, kb, _ in jobs]
        sp = [jnp.maximum(zi, 0.0) + jnp.log(1.0 + jnp.exp(-jnp.abs(zi))) for zi in z]
        spb = [(jnp.where(strictly_causal, si, 0.0) if job[3] else si).astype(BF16) for si, job in zip(sp, jobs)]
        run_a = [jnp.dot(si[:, :n], later_and_total, preferred_element_type=F32) for si in spb]
        run_b = [jnp.dot(si[:, n:], later_and_total, preferred_element_type=F32) for si in spb]
        a = []
        for i, (u, _, _, diagonal) in enumerate(jobs):
            suffix = jnp.concatenate([run_a[i][:, :n], run_b[i][:, :n]], axis=1) + c[u]
            c[u] = c[u] + jnp.concatenate([run_a[i][:, n:], run_b[i][:, n:]], axis=1)
            ai = jnp.exp(z[i] - sp[i] - suffix)
            if diagonal:
                ai = jnp.where(strictly_causal, ai, 0.0)
            a.append(ai.astype(BF16))
        for i, (u, _, kb, _) in enumerate(jobs):
            acc[u] = acc[u] + jnp.dot(a[i], v2[id(kb)], preferred_element_type=F32)

    def key_block(q, kb, diagonal, c, acc):
        c, acc = [c], [acc]
        key_blocks([(0, q, kb, diagonal)], c, acc)
        return c[0], acc[0]

    def query_tiles(base, n_tiles, n_fast):
        blocks = {off: base + off for off in range(-n_fast, n_tiles)}
        jobs = []
        for u in range(n_tiles):
            q = q_ref[pl.ds(pl.multiple_of(blocks[u] * n, n), n), :]
            jobs += [(u, q, blocks[u - j], j == 0) for j in range(n_fast + 1)]
        c = [jnp.zeros((n, 2 * n), F32)] * n_tiles
        acc = [jnp.zeros((n, LANES), F32)] * n_tiles
        key_blocks(jobs, c, acc)
        c_all = None
        for u in range(n_tiles):
            c_ref[u] = c[u]
            acc_ref[u] = acc[u]
            c_all = c[u] if c_all is None else jnp.minimum(c_all, c[u])

        @pl.when(jnp.min(c_all) < ATT_ZERO_WEIGHT)
        def _():
            for u in range(n_tiles):
                qb = base + u
                q = q_ref[pl.ds(pl.multiple_of(qb * n, n), n), :]

                def more(carry):
                    kb, c_min = carry
                    return jnp.logical_and(kb >= 0, c_min < ATT_ZERO_WEIGHT)

                def step(carry, u=u, q=q):
                    kb, _ = carry
                    c, acc = key_block(q, kb, False, c_ref[u], acc_ref[u])
                    c_ref[u] = c
                    acc_ref[u] = acc
                    return kb - 1, jnp.min(c)

                lax.while_loop(more, step, (qb - n_fast - 1, jnp.min(c_ref[u])))

        for u in range(n_tiles):
            q0 = pl.multiple_of((base + u) * n, n)
            o_ref[pl.ds(q0, n), :] = acc_ref[u].astype(o_ref.dtype)

    def head_tile(qb, carry):
        query_tiles(qb, 1, 0)
        return carry

    def body_tiles(g, carry):
        query_tiles(n_head + g * ATT_UNROLL, ATT_UNROLL, ATT_FAST_BLOCKS)
        return carry

    n_total = s // n
    n_head = ATT_FAST_BLOCKS + (n_total - ATT_FAST_BLOCKS) % ATT_UNROLL
    lax.fori_loop(0, n_head, head_tile, 0)
    lax.fori_loop(0, (n_total - n_head) // ATT_UNROLL, body_tiles, 0)


def _attention(qkv, d):
    s = qkv.shape[0]
    nb = d // LANES
    return pl.pallas_call(
        _attn_kernel,
        grid=(nb,),
        in_specs=[
            pl.BlockSpec((s, LANES), lambda hb: (0, hb)),
            pl.BlockSpec((s, LANES), lambda hb: (0, nb + hb)),
            pl.BlockSpec((s, LANES), lambda hb: (0, 2 * nb + hb)),
        ],
        out_specs=pl.BlockSpec((s, LANES), lambda hb: (0, hb)),
        out_shape=jax.ShapeDtypeStruct((s, d), BF16),
        scratch_shapes=[pltpu.VMEM((ATT_UNROLL, ATT_TILE, 2 * ATT_TILE), F32),
                        pltpu.VMEM((ATT_UNROLL, ATT_TILE, LANES), F32)],
        compiler_params=_params(("arbitrary",)),
        name="stickbreak_attn",
    )(qkv, qkv, qkv)


def _pool_kernel(x_ref, g_ref, mod_ref, pw_ref, pb_ref, ps_ref, o_ref, h_ref):
    tm, d = x_ref.shape
    gd = d // len(POOL_WINDOWS)
    i = pl.program_id(0)

    @pl.when(i == 0)
    def _():
        h_ref[0:POOL_HALO, :] = jnp.zeros((POOL_HALO, d), F32)

    @pl.when(i > 0)
    def _():
        h_ref[0:POOL_HALO, :] = h_ref[tm:tm + POOL_HALO, :]

    x = x_ref[...]
    h = _modulate(x, g_ref[...], mod_ref)
    h_ref[POOL_HALO:POOL_HALO + tm, :] = h
    t_pos = i * tm + lax.broadcasted_iota(jnp.int32, (tm, 1), 0)
    gate = 1.0 + mod_ref[2:3, :]
    for gi, w in enumerate(POOL_WINDOWS):
        cs = slice(gi * gd, (gi + 1) * gd)
        win = h[:, cs]
        for j in range(1, w):
            win = win + h_ref[POOL_HALO - j:POOL_HALO - j + tm, cs]
        cnt = jnp.minimum(t_pos + 1, w).astype(F32)
        diff = win / cnt - h[:, cs]
        y = jnp.dot(diff.astype(BF16), pw_ref[gi], preferred_element_type=F32) + pb_ref[gi:gi + 1, :]
        o_ref[:, cs] = x[:, cs] + gate[:, cs] * (y * ps_ref[:, cs])


def _pool_mixer(x, g, mod, p_w, p_b, p_scale):
    s, d = x.shape
    ng, gd, _ = p_w.shape
    return pl.pallas_call(
        _pool_kernel,
        grid=(s // ROW_TILE,),
        in_specs=[
            _row_spec(ROW_TILE, d),
            _resident((1, d)),
            _resident((3, d)),
            _resident((ng, gd, gd)),
            _resident((ng, gd)),
            _resident((1, d)),
        ],
        out_specs=_row_spec(ROW_TILE, d),
        out_shape=jax.ShapeDtypeStruct((s, d), F32),
        scratch_shapes=[pltpu.VMEM((POOL_HALO + ROW_TILE, d), F32)],
        compiler_params=_params(("arbitrary",)),
        name="pool_mixer",
    )(x, g.reshape(1, d), mod, p_w.astype(BF16), p_b, p_scale.reshape(1, d))


def kernel(x, c, cond_w, cond_b, ada_w, ada_b, norm_g, ffn_w1, ffn_w3, ffn_w2, a_w_in, a_b_in, a_dw, a_dw_b, a_ln_g, a_ln_b, a_w_out, a_b_out, b_w_qkv, b_q_g, b_k_g, b_w_o, p_w, p_b, p_scale):
    batch, s, d = x.shape
    assert batch == 1 and d == N_HEADS * HEAD_DIM
    assert s % ROW_TILE == 0 and s % ATT_TILE == 0 and s // ATT_TILE >= ATT_FAST_BLOCKS + ATT_UNROLL
    depth = ada_w.shape[0]
    mod = _conditioning(c, cond_w, cond_b, ada_w, ada_b)
    xs = x.reshape(s, d)
    stacks = (ffn_w1, ffn_w3, ffn_w2)
    weights = tuple(w[0, 0].astype(BF16) for w in stacks)
    ia = ib = ic = 0
    for i in range(depth):
        xs, weights = _ffn(xs, norm_g[i, 0], mod[i, 0], weights, [(w, (i, 1)) for w in stacks])
        kind = i % 3
        proj = None
        if kind == 0:
            xs = _conv_mixer(xs, norm_g[i, 1], mod[i, 1], a_w_in[ia], a_b_in[ia], a_dw[ia], a_dw_b[ia],
                             a_ln_g[ia], a_ln_b[ia], a_w_out[ia], a_b_out[ia])
            ia += 1
        elif kind == 1:
            qkv = _qkv_proj(xs, norm_g[i, 1], mod[i, 1], b_w_qkv[ib], b_q_g[ib], b_k_g[ib])
            proj = (_attention(qkv, d), mod[i, 1], b_w_o[ib].astype(BF16))
            ib += 1
        else:
            xs = _pool_mixer(xs, norm_g[i, 1], mod[i, 1], p_w[ic], p_b[ic], p_scale[ic])
            ic += 1
        following = [(w, (i + 1, 0)) for w in stacks] if i + 1 < depth else None
        xs, weights = _ffn(xs, norm_g[i, 2], mod[i, 2], weights, following, proj)
    return xs.reshape(batch, s, d)
```

```python
import functools

import jax
import jax.numpy as jnp
from jax import lax
from jax.experimental import pallas as pl
from jax.experimental.pallas import tpu as pltpu

F32 = jnp.float32
BF16 = jnp.bfloat16

EPS = 1e-6
N_HEADS = 16
HEAD_DIM = 64
CONV_WIDTH = 31
POOL_WINDOWS = (2, 4, 8, 16)
LANES = 128
SUBLANES = 8
HEADS_PER_BLOCK = LANES // HEAD_DIM

ROW_TILE = 512
FFN_TILE = 1024
FF_CHUNK = 256
CONV_HALO = 32
CONV_SUB = 128
POOL_HALO = 32
ATT_TILE = 128
ATT_FAST_BLOCKS = 2
ATT_UNROLL = 14
ATT_ZERO_WEIGHT = 104.0
VMEM_LIMIT = 56 * 1024 * 1024


def _sigmoid(x):
    return 1.0 / (1.0 + jnp.exp(-x))


def _modulate(x, g, mod_ref):
    ms = jnp.mean(x * x, axis=-1, keepdims=True)
    gs = g * (1.0 + mod_ref[1:2, :])
    return (x * lax.rsqrt(ms + EPS)) * gs + mod_ref[0:1, :]


def _resident(shape):
    zeros = (0,) * len(shape)
    return pl.BlockSpec(shape, lambda *_: zeros, pipeline_mode=pl.Buffered(1))


def _row_spec(tile, width):
    return pl.BlockSpec((tile, width), lambda i: (i, 0))


def _params(semantics):
    return pltpu.CompilerParams(dimension_semantics=semantics, vmem_limit_bytes=VMEM_LIMIT)


def _cond_embed_kernel(c_ref, w_ref, b_ref, e_ref):
    t = jnp.sum(c_ref[...] * w_ref[...], axis=0, keepdims=True) + b_ref[...]
    e_ref[...] = t * _sigmoid(t)


def _cond_mod_kernel(e_ref, w_ref, b_ref, o_ref):
    o_ref[0] = jnp.sum(e_ref[...] * w_ref[0], axis=0, keepdims=True) + b_ref[0]


def _conditioning(c, cond_w, cond_b, ada_w, ada_b):
    d = cond_w.shape[0]
    depth, _, n = ada_w.shape
    e = pl.pallas_call(
        _cond_embed_kernel,
        out_shape=jax.ShapeDtypeStruct((1, d), F32),
        name="cond_embed",
    )(c.reshape(d, 1), cond_w, cond_b.reshape(1, d))
    tn = n // 8
    mod = pl.pallas_call(
        _cond_mod_kernel,
        grid=(depth, n // tn),
        in_specs=[
            pl.BlockSpec((d, 1), lambda l, j: (0, 0)),
            pl.BlockSpec((1, d, tn), lambda l, j: (l, 0, j)),
            pl.BlockSpec((1, 1, tn), lambda l, j: (l, 0, j)),
        ],
        out_specs=pl.BlockSpec((1, 1, tn), lambda l, j: (l, 0, j)),
        out_shape=jax.ShapeDtypeStruct((depth, 1, n), F32),
        compiler_params=_params(("arbitrary", "arbitrary")),
        name="cond_mod",
    )(e.reshape(d, 1), ada_w, ada_b.reshape(depth, 1, n))
    return mod.reshape(depth, 3, 3, d)


def _ffn_stages(x_ref, g_ref, mod_ref, w1_ref, w3_ref, w2_ref, o_ref, h_ref, acc_ref):
    def start():
        h_ref[...] = _modulate(x_ref[...], g_ref[...], mod_ref).astype(BF16)
        acc_ref[...] = jnp.zeros_like(acc_ref)

    def chunk(f):
        fs = slice(f * FF_CHUNK, (f + 1) * FF_CHUNK)
        h = h_ref[...]
        a = jnp.dot(h, w1_ref[:, fs], preferred_element_type=F32)
        b = jnp.dot(h, w3_ref[:, fs], preferred_element_type=F32)
        act = (a * _sigmoid(a) * b).astype(BF16)
        acc_ref[...] += jnp.dot(act, w2_ref[fs, :], preferred_element_type=F32)

    def finish():
        o_ref[...] = x_ref[...] + (0.5 * (1.0 + mod_ref[2:3, :])) * acc_ref[...]

    return start, chunk, finish


def _ffn_kernel(*refs, n_cast, has_proj):
    refs = list(refs)
    x_ref, g_ref, mod_ref, w1_ref, w3_ref, w2_ref = refs[:6]
    del refs[:6]
    if has_proj:
        attn_ref, pmod_ref, wo_ref = refs[:3]
        del refs[:3]
    cast_in = refs[:n_cast]
    o_ref = refs[n_cast]
    cast_out = refs[n_cast + 1:2 * n_cast + 1]
    h_ref, acc_ref = refs[2 * n_cast + 1:2 * n_cast + 3]

    for src_ref, dst_ref in zip(cast_in, cast_out):
        dst_ref[...] = src_ref[...].astype(BF16)

    if has_proj:
        xin_ref = refs[-1]
        y = jnp.dot(attn_ref[...], wo_ref[...], preferred_element_type=F32)
        xin_ref[...] = x_ref[...] + (1.0 + pmod_ref[2:3, :]) * y
    else:
        xin_ref = x_ref

    start, chunk, finish = _ffn_stages(xin_ref, g_ref, mod_ref, w1_ref, w3_ref, w2_ref, o_ref, h_ref, acc_ref)
    start()
    for f in range(w1_ref.shape[1] // FF_CHUNK):
        chunk(f)
    finish()


def _ffn(x, g, mod, weights, next_weights=None, proj=None):
    s, d = x.shape
    w1, w3, w2 = weights
    f = w1.shape[-1]
    steps = s // FFN_TILE
    assert f % FF_CHUNK == 0 and s % FFN_TILE == 0
    in_specs = [_row_spec(FFN_TILE, d), _resident((1, d)), _resident((3, d)),
                _resident(w1.shape), _resident(w3.shape), _resident(w2.shape)]
    args = [x, g.reshape(1, d), mod, w1, w3, w2]
    scratch = [pltpu.VMEM((FFN_TILE, d), BF16), pltpu.VMEM((FFN_TILE, d), F32)]
    if proj is not None:
        attn, pmod, w_o = proj
        in_specs += [_row_spec(FFN_TILE, d), _resident((3, d)), _resident((d, d))]
        args += [attn, pmod, w_o]
        scratch.append(pltpu.VMEM((FFN_TILE, d), F32))
    out_specs = [_row_spec(FFN_TILE, d)]
    out_shape = [jax.ShapeDtypeStruct((s, d), F32)]
    for stack, index in next_weights or ():
        rows, cols = stack.shape[-2:]
        slab = rows // steps
        assert rows % steps == 0 and slab % (2 * SUBLANES) == 0
        lead = tuple(index)
        in_specs.append(pl.BlockSpec((None,) * len(lead) + (slab, cols), lambda i, lead=lead: lead + (i, 0)))
        args.append(stack)
        out_specs.append(pl.BlockSpec((slab, cols), lambda i: (i, 0)))
        out_shape.append(jax.ShapeDtypeStruct((rows, cols), BF16))
    n_cast = len(out_specs) - 1
    out = pl.pallas_call(
        functools.partial(_ffn_kernel, n_cast=n_cast, has_proj=proj is not None),
        grid=(steps,),
        in_specs=in_specs,
        out_specs=out_specs,
        out_shape=out_shape,
        scratch_shapes=scratch,
        compiler_params=_params(("arbitrary",)),
        name="ffn",
    )(*args)
    return out[0], tuple(out[1:])


def _conv_stages(x_ref, g_ref, mod_ref, win_ref, bin_ref, dw_ref, dwb_ref, lng_ref, lnb_ref,
                 wout_ref, bout_ref, o_ref, u_ref, us_ref, v_ref):
    d = x_ref.shape[1]
    gate = 1.0 + mod_ref[2:3, :]
    base = CONV_HALO - (CONV_WIDTH - 1)

    def glu_in(k):
        rs = slice(k * CONV_SUB, (k + 1) * CONV_SUB)
        h = _modulate(x_ref[rs, :], g_ref[...], mod_ref).astype(BF16)
        uv = jnp.dot(h, win_ref[...], preferred_element_type=F32) + bin_ref[...]
        u_ref[CONV_HALO + k * CONV_SUB:CONV_HALO + (k + 1) * CONV_SUB, :] = uv[:, :d] * _sigmoid(uv[:, d:])

    def shifted(k):
        lo = k * CONV_SUB
        for b in range(1, SUBLANES):
            us_ref[b - 1] = u_ref[lo + b:lo + b + us_ref.shape[1], :]

    def depthwise(k, c):
        lo = k * CONV_SUB
        cs = slice(c * LANES, (c + 1) * LANES)
        taps = [jnp.broadcast_to(dw_ref[j:j + 1, cs], (SUBLANES, LANES)) for j in range(CONV_WIDTH)]
        bias = jnp.broadcast_to(dwb_ref[:, cs], (SUBLANES, LANES))
        for r in range(0, CONV_SUB, SUBLANES):
            acc = bias
            for j in range(CONV_WIDTH):
                a, b = divmod(base + j, SUBLANES)
                r0 = r + a * SUBLANES
                src = u_ref[lo + r0:lo + r0 + SUBLANES, cs] if b == 0 else us_ref[b - 1, r0:r0 + SUBLANES, cs]
                acc = acc + taps[j] * src
            v_ref[r:r + SUBLANES, cs] = acc

    def norm_out(k):
        rs = slice(k * CONV_SUB, (k + 1) * CONV_SUB)
        v = v_ref[...]
        mu = jnp.mean(v, axis=-1, keepdims=True)
        vc = v - mu
        var = jnp.mean(vc * vc, axis=-1, keepdims=True)
        y = vc * lax.rsqrt(var + EPS) * lng_ref[...] + lnb_ref[...]
        y = (y * _sigmoid(y)).astype(BF16)
        y = jnp.dot(y, wout_ref[...], preferred_element_type=F32) + bout_ref[...]
        o_ref[rs, :] = x_ref[rs, :] + gate * y

    return glu_in, shifted, depthwise, norm_out


def _conv_kernel(x_ref, g_ref, mod_ref, win_ref, bin_ref, dw_ref, dwb_ref, lng_ref, lnb_ref,
                 wout_ref, bout_ref, o_ref, u_ref, us_ref, v_ref):
    tm, d = x_ref.shape
    i = pl.program_id(0)

    @pl.when(i == 0)
    def _():
        u_ref[0:CONV_HALO, :] = jnp.zeros((CONV_HALO, d), F32)

    @pl.when(i > 0)
    def _():
        u_ref[0:CONV_HALO, :] = u_ref[tm:tm + CONV_HALO, :]

    glu_in, shifted, depthwise, norm_out = _conv_stages(
        x_ref, g_ref, mod_ref, win_ref, bin_ref, dw_ref, dwb_ref, lng_ref, lnb_ref, wout_ref, bout_ref,
        o_ref, u_ref, us_ref, v_ref)

    n_sub = tm // CONV_SUB
    glu_in(0)
    for k in range(n_sub):
        if k + 1 < n_sub:
            glu_in(k + 1)
        shifted(k)
        for c in range(d // LANES):
            depthwise(k, c)
        norm_out(k)


def _conv_mixer(x, g, mod, w_in, b_in, dw, dw_b, ln_g, ln_b, w_out, b_out):
    s, d = x.shape
    assert s % ROW_TILE == 0 and ROW_TILE % CONV_SUB == 0
    row = lambda v: v.reshape(1, -1)
    return pl.pallas_call(
        _conv_kernel,
        grid=(s // ROW_TILE,),
        in_specs=[
            _row_spec(ROW_TILE, d),
            _resident((1, d)),
            _resident((3, d)),
            _resident((d, 2 * d)),
            _resident((1, 2 * d)),
            _resident((CONV_WIDTH, d)),
            _resident((1, d)),
            _resident((1, d)),
            _resident((1, d)),
            _resident((d, d)),
            _resident((1, d)),
        ],
        out_specs=_row_spec(ROW_TILE, d),
        out_shape=jax.ShapeDtypeStruct((s, d), F32),
        scratch_shapes=[
            pltpu.VMEM((CONV_HALO + ROW_TILE, d), F32),
            pltpu.VMEM((SUBLANES - 1, CONV_SUB + CONV_HALO - SUBLANES, d), F32),
            pltpu.VMEM((CONV_SUB, d), F32),
        ],
        compiler_params=_params(("arbitrary",)),
        name="conv_mixer",
    )(x, row(g), mod, w_in.astype(BF16), row(b_in), dw, row(dw_b), row(ln_g), row(ln_b),
      w_out.astype(BF16), row(b_out))


def _qkv_kernel(x_ref, g_ref, mod_ref, w_ref, qg_ref, kg_ref, o_ref):
    tm, d = x_ref.shape
    h = _modulate(x_ref[...], g_ref[...], mod_ref).astype(BF16)
    qkv = jnp.dot(h, w_ref[...], preferred_element_type=F32)
    first = lax.broadcasted_iota(jnp.int32, (1, LANES), 1) < HEAD_DIM

    def head_norm(blk, gain):
        sq = blk * blk
        ms_a = jnp.sum(jnp.where(first, sq, 0.0), axis=-1, keepdims=True) * (1.0 / HEAD_DIM)
        ms_b = jnp.sum(jnp.where(first, 0.0, sq), axis=-1, keepdims=True) * (1.0 / HEAD_DIM)
        r = jnp.where(first, lax.rsqrt(ms_a + EPS), lax.rsqrt(ms_b + EPS))
        return blk * r * gain

    nb = d // LANES
    for j in range(nb):
        cs = slice(j * LANES, (j + 1) * LANES)
        o_ref[:, cs] = (head_norm(qkv[:, cs], qg_ref[...]) * (HEAD_DIM ** -0.5)).astype(BF16)
    for j in range(nb, 2 * nb):
        cs = slice(j * LANES, (j + 1) * LANES)
        o_ref[:, cs] = head_norm(qkv[:, cs], kg_ref[...]).astype(BF16)
    o_ref[:, 2 * d:] = qkv[:, 2 * d:].astype(BF16)


def _qkv_proj(x, g, mod, w_qkv, q_g, k_g):
    s, d = x.shape
    tile_gain = lambda v: jnp.tile(v, HEADS_PER_BLOCK).reshape(1, LANES)
    return pl.pallas_call(
        _qkv_kernel,
        grid=(s // ROW_TILE,),
        in_specs=[
            _row_spec(ROW_TILE, d),
            _resident((1, d)),
            _resident((3, d)),
            _resident((d, 3 * d)),
            _resident((1, LANES)),
            _resident((1, LANES)),
        ],
        out_specs=_row_spec(ROW_TILE, 3 * d),
        out_shape=jax.ShapeDtypeStruct((s, 3 * d), BF16),
        compiler_params=_params(("arbitrary",)),
        name="qkv_proj",
    )(x, g.reshape(1, d), mod, w_qkv.astype(BF16), tile_gain(q_g), tile_gain(k_g))


def _attn_kernel(q_ref, k_ref, v_ref, o_ref, c_ref, acc_ref):
    n = ATT_TILE
    s = q_ref.shape[0]
    first = lax.broadcasted_iota(jnp.int32, (1, LANES), 1) < HEAD_DIM
    row = lax.broadcasted_iota(jnp.int32, (n, n), 0)
    col = lax.broadcasted_iota(jnp.int32, (n, n), 1)
    later_and_total = jnp.concatenate([(row > col).astype(BF16), jnp.ones((n, n), BF16)], axis=1)
    strictly_causal = jnp.concatenate([col < row, col < row], axis=1)
    zero = jnp.zeros((), BF16)

    def key_blocks(jobs, c, acc):
        k2, v2 = {}, {}
        for _, _, kb, _ in jobs:
            if id(kb) not in k2:
                k0 = pl.multiple_of(kb * n, n)
                k = k_ref[pl.ds(k0, n), :]
                v = v_ref[pl.ds(k0, n), :]
                k2[id(kb)] = jnp.concatenate([jnp.where(first, k, zero), jnp.where(first, zero, k)], axis=0)
                v2[id(kb)] = jnp.concatenate([jnp.where(first, v, zero), jnp.where(first, zero, v)], axis=0)
        z = [lax.dot_general(q, k2[id(kb)], (((1,), (1,)), ((), ())), preferred_element_type=F32)
             for _, q, kb, _ in jobs]
        sp = [jnp.maximum(zi, 0.0) + jnp.log(1.0 + jnp.exp(-jnp.abs(zi))) for zi in z]
        spb = [(jnp.where(strictly_causal, si, 0.0) if job[3] else si).astype(BF16) for si, job in zip(sp, jobs)]
        run_a = [jnp.dot(si[:, :n], later_and_total, preferred_element_type=F32) for si in spb]
        run_b = [jnp.dot(si[:, n:], later_and_total, preferred_element_type=F32) for si in spb]
        a = []
        for i, (u, _, _, diagonal) in enumerate(jobs):
            suffix = jnp.concatenate([run_a[i][:, :n], run_b[i][:, :n]], axis=1) + c[u]
            c[u] = c[u] + jnp.concatenate([run_a[i][:, n:], run_b[i][:, n:]], axis=1)
            ai = jnp.exp(z[i] - sp[i] - suffix)
            if diagonal:
                ai = jnp.where(strictly_causal, ai, 0.0)
            a.append(ai.astype(BF16))
        for i, (u, _, kb, _) in enumerate(jobs):
            acc[u] = acc[u] + jnp.dot(a[i], v2[id(kb)], preferred_element_type=F32)

    def key_block(q, kb, diagonal, c, acc):
        c, acc = [c], [acc]
        key_blocks([(0, q, kb, diagonal)], c, acc)
        return c[0], acc[0]

    def query_tiles(base, n_tiles, n_fast):
        blocks = {off: base + off for off in range(-n_fast, n_tiles)}
        jobs = []
        for u in range(n_tiles):
            q = q_ref[pl.ds(pl.multiple_of(blocks[u] * n, n), n), :]
            jobs += [(u, q, blocks[u - j], j == 0) for j in range(n_fast + 1)]
        c = [jnp.zeros((n, 2 * n), F32)] * n_tiles
        acc = [jnp.zeros((n, LANES), F32)] * n_tiles
        key_blocks(jobs, c, acc)
        c_all = None
        for u in range(n_tiles):
            c_ref[u] = c[u]
            acc_ref[u] = acc[u]
            c_all = c[u] if c_all is None else jnp.minimum(c_all, c[u])

        @pl.when(jnp.min(c_all) < ATT_ZERO_WEIGHT)
        def _():
            for u in range(n_tiles):
                qb = base + u
                q = q_ref[pl.ds(pl.multiple_of(qb * n, n), n), :]

                def more(carry):
                    kb, c_min = carry
                    return jnp.logical_and(kb >= 0, c_min < ATT_ZERO_WEIGHT)

                def step(carry, u=u, q=q):
                    kb, _ = carry
                    c, acc = key_block(q, kb, False, c_ref[u], acc_ref[u])
                    c_ref[u] = c
                    acc_ref[u] = acc
                    return kb - 1, jnp.min(c)

                lax.while_loop(more, step, (qb - n_fast - 1, jnp.min(c_ref[u])))

        for u in range(n_tiles):
            q0 = pl.multiple_of((base + u) * n, n)
            o_ref[pl.ds(q0, n), :] = acc_ref[u].astype(o_ref.dtype)

    def head_tile(qb, carry):
        query_tiles(qb, 1, 0)
        return carry

    def body_tiles(g, carry):
        query_tiles(n_head + g * ATT_UNROLL, ATT_UNROLL, ATT_FAST_BLOCKS)
        return carry

    n_total = s // n
    n_head = ATT_FAST_BLOCKS + (n_total - ATT_FAST_BLOCKS) % ATT_UNROLL
    lax.fori_loop(0, n_head, head_tile, 0)
    lax.fori_loop(0, (n_total - n_head) // ATT_UNROLL, body_tiles, 0)


def _attention(qkv, d):
    s = qkv.shape[0]
    nb = d // LANES
    return pl.pallas_call(
        _attn_kernel,
        grid=(nb,),
        in_specs=[
            pl.BlockSpec((s, LANES), lambda hb: (0, hb)),
            pl.BlockSpec((s, LANES), lambda hb: (0, nb + hb)),
            pl.BlockSpec((s, LANES), lambda hb: (0, 2 * nb + hb)),
        ],
        out_specs=pl.BlockSpec((s, LANES), lambda hb: (0, hb)),
        out_shape=jax.ShapeDtypeStruct((s, d), BF16),
        scratch_shapes=[pltpu.VMEM((ATT_UNROLL, ATT_TILE, 2 * ATT_TILE), F32),
                        pltpu.VMEM((ATT_UNROLL, ATT_TILE, LANES), F32)],
        compiler_params=_params(("arbitrary",)),
        name="stickbreak_attn",
    )(qkv, qkv, qkv)


def _pool_kernel(x_ref, g_ref, mod_ref, pw_ref, pb_ref, ps_ref, o_ref, h_ref, lvl_ref):
    tm, d = x_ref.shape
    n_g = len(POOL_WINDOWS)
    gd = d // n_g
    i = pl.program_id(0)

    @pl.when(i == 0)
    def _():
        h_ref[0:POOL_HALO, :] = jnp.zeros((POOL_HALO, d), F32)

    @pl.when(i > 0)
    def _():
        h_ref[0:POOL_HALO, :] = h_ref[tm:tm + POOL_HALO, :]

    x = x_ref[...]
    h = _modulate(x, g_ref[...], mod_ref)
    h_ref[POOL_HALO:POOL_HALO + tm, :] = h
    t_pos = i * tm + lax.broadcasted_iota(jnp.int32, (tm, 1), 0)
    gate = 1.0 + mod_ref[2:3, :]

    rows = POOL_HALO + tm
    for m in range(n_g):
        back = 2 ** m
        first = SUBLANES * (m + 1)
        lanes = slice(m * gd, d)
        prev_ref = h_ref if m == 0 else lvl_ref.at[m - 1]
        level = prev_ref[first:rows, lanes] + prev_ref[first - back:rows - back, lanes]
        if m + 1 < n_g:
            lvl_ref[m, first:rows, lanes] = level
        cs = slice(m * gd, (m + 1) * gd)
        win = level[POOL_HALO - first:, 0:gd]
        cnt = jnp.minimum(t_pos + 1, POOL_WINDOWS[m]).astype(F32)
        diff = win / cnt - h[:, cs]
        y = jnp.dot(diff.astype(BF16), pw_ref[m], preferred_element_type=F32) + pb_ref[m:m + 1, :]
        o_ref[:, cs] = x[:, cs] + gate[:, cs] * (y * ps_ref[:, cs])


def _pool_mixer(x, g, mod, p_w, p_b, p_scale):
    s, d = x.shape
    ng, gd, _ = p_w.shape
    assert POOL_WINDOWS == tuple(2 ** (m + 1) for m in range(ng)) and POOL_HALO >= SUBLANES * ng
    return pl.pallas_call(
        _pool_kernel,
        grid=(s // ROW_TILE,),
        in_specs=[
            _row_spec(ROW_TILE, d),
            _resident((1, d)),
            _resident((3, d)),
            _resident((ng, gd, gd)),
            _resident((ng, gd)),
            _resident((1, d)),
        ],
        out_specs=_row_spec(ROW_TILE, d),
        out_shape=jax.ShapeDtypeStruct((s, d), F32),
        scratch_shapes=[pltpu.VMEM((POOL_HALO + ROW_TILE, d), F32),
                        pltpu.VMEM((ng - 1, POOL_HALO + ROW_TILE, d), F32)],
        compiler_params=_params(("arbitrary",)),
        name="pool_mixer",
    )(x, g.reshape(1, d), mod, p_w.astype(BF16), p_b, p_scale.reshape(1, d))


def kernel(x, c, cond_w, cond_b, ada_w, ada_b, norm_g, ffn_w1, ffn_w3, ffn_w2, a_w_in, a_b_in, a_dw, a_dw_b, a_ln_g, a_ln_b, a_w_out, a_b_out, b_w_qkv, b_q_g, b_k_g, b_w_o, p_w, p_b, p_scale):
    batch, s, d = x.shape
    assert batch == 1 and d == N_HEADS * HEAD_DIM
    assert s % ROW_TILE == 0 and s % ATT_TILE == 0 and s // ATT_TILE >= ATT_FAST_BLOCKS + ATT_UNROLL
    depth = ada_w.shape[0]
    mod = _conditioning(c, cond_w, cond_b, ada_w, ada_b)
    xs = x.reshape(s, d)
    stacks = (ffn_w1, ffn_w3, ffn_w2)
    weights = tuple(w[0, 0].astype(BF16) for w in stacks)
    ia = ib = ic = 0
    for i in range(depth):
        xs, weights = _ffn(xs, norm_g[i, 0], mod[i, 0], weights, [(w, (i, 1)) for w in stacks])
        kind = i % 3
        proj = None
        if kind == 0:
            xs = _conv_mixer(xs, norm_g[i, 1], mod[i, 1], a_w_in[ia], a_b_in[ia], a_dw[ia], a_dw_b[ia],
                             a_ln_g[ia], a_ln_b[ia], a_w_out[ia], a_b_out[ia])
            ia += 1
        elif kind == 1:
            qkv = _qkv_proj(xs, norm_g[i, 1], mod[i, 1], b_w_qkv[ib], b_q_g[ib], b_k_g[ib])
            proj = (_attention(qkv, d), mod[i, 1], b_w_o[ib].astype(BF16))
            ib += 1
        else:
            xs = _pool_mixer(xs, norm_g[i, 1], mod[i, 1], p_w[ic], p_b[ic], p_scale[ic])
            ic += 1
        following = [(w, (i + 1, 0)) for w in stacks] if i + 1 < depth else None
        xs, weights = _ffn(xs, norm_g[i, 2], mod[i, 2], weights, following, proj)
    return xs.reshape(batch, s, d)
```

```python
import functools

import jax
import jax.numpy as jnp
from jax import lax
from jax.experimental import pallas as pl
from jax.experimental.pallas import tpu as pltpu

F32 = jnp.float32
BF16 = jnp.bfloat16

EPS = 1e-6
N_HEADS = 16
HEAD_DIM = 64
CONV_WIDTH = 31
POOL_WINDOWS = (2, 4, 8, 16)
LANES = 128
SUBLANES = 8
HEADS_PER_BLOCK = LANES // HEAD_DIM

ROW_TILE = 512
FFN_TILE = 1024
FF_CHUNK = 256
CONV_HALO = 32
CONV_SUB = 256
POOL_HALO = 32
ATT_TILE = 128
ATT_FAST_BLOCKS = 2
ATT_UNROLL = 14
ATT_ZERO_WEIGHT = 104.0
VMEM_LIMIT = 56 * 1024 * 1024


def _sigmoid(x):
    return 1.0 / (1.0 + jnp.exp(-x))


def _modulate(x, g, mod_ref):
    ms = jnp.mean(x * x, axis=-1, keepdims=True)
    gs = g * (1.0 + mod_ref[1:2, :])
    return (x * lax.rsqrt(ms + EPS)) * gs + mod_ref[0:1, :]


def _resident(shape):
    zeros = (0,) * len(shape)
    return pl.BlockSpec(shape, lambda *_: zeros, pipeline_mode=pl.Buffered(1))


def _row_spec(tile, width):
    return pl.BlockSpec((tile, width), lambda i: (i, 0))


def _params(semantics):
    return pltpu.CompilerParams(dimension_semantics=semantics, vmem_limit_bytes=VMEM_LIMIT)


def _cond_embed_kernel(c_ref, w_ref, b_ref, e_ref):
    t = jnp.sum(c_ref[...] * w_ref[...], axis=0, keepdims=True) + b_ref[...]
    e_ref[...] = t * _sigmoid(t)


def _cond_mod_kernel(e_ref, w_ref, b_ref, o_ref):
    o_ref[0] = jnp.sum(e_ref[...] * w_ref[0], axis=0, keepdims=True) + b_ref[0]


def _conditioning(c, cond_w, cond_b, ada_w, ada_b):
    d = cond_w.shape[0]
    depth, _, n = ada_w.shape
    e = pl.pallas_call(
        _cond_embed_kernel,
        out_shape=jax.ShapeDtypeStruct((1, d), F32),
        name="cond_embed",
    )(c.reshape(d, 1), cond_w, cond_b.reshape(1, d))
    tn = n // 8
    mod = pl.pallas_call(
        _cond_mod_kernel,
        grid=(depth, n // tn),
        in_specs=[
            pl.BlockSpec((d, 1), lambda l, j: (0, 0)),
            pl.BlockSpec((1, d, tn), lambda l, j: (l, 0, j)),
            pl.BlockSpec((1, 1, tn), lambda l, j: (l, 0, j)),
        ],
        out_specs=pl.BlockSpec((1, 1, tn), lambda l, j: (l, 0, j)),
        out_shape=jax.ShapeDtypeStruct((depth, 1, n), F32),
        compiler_params=_params(("arbitrary", "arbitrary")),
        name="cond_mod",
    )(e.reshape(d, 1), ada_w, ada_b.reshape(depth, 1, n))
    return mod.reshape(depth, 3, 3, d)


def _ffn_stages(x_ref, g_ref, mod_ref, w1_ref, w3_ref, w2_ref, o_ref, h_ref, acc_ref):
    def start():
        h_ref[...] = _modulate(x_ref[...], g_ref[...], mod_ref).astype(BF16)
        acc_ref[...] = jnp.zeros_like(acc_ref)

    def chunk(f):
        fs = slice(f * FF_CHUNK, (f + 1) * FF_CHUNK)
        h = h_ref[...]
        a = jnp.dot(h, w1_ref[:, fs], preferred_element_type=F32)
        b = jnp.dot(h, w3_ref[:, fs], preferred_element_type=F32)
        act = (a * _sigmoid(a) * b).astype(BF16)
        acc_ref[...] += jnp.dot(act, w2_ref[fs, :], preferred_element_type=F32)

    def finish():
        o_ref[...] = x_ref[...] + (0.5 * (1.0 + mod_ref[2:3, :])) * acc_ref[...]

    return start, chunk, finish


def _ffn_kernel(*refs, n_cast, has_proj):
    refs = list(refs)
    x_ref, g_ref, mod_ref, w1_ref, w3_ref, w2_ref = refs[:6]
    del refs[:6]
    if has_proj:
        attn_ref, pmod_ref, wo_ref = refs[:3]
        del refs[:3]
    cast_in = refs[:n_cast]
    o_ref = refs[n_cast]
    cast_out = refs[n_cast + 1:2 * n_cast + 1]
    h_ref, acc_ref = refs[2 * n_cast + 1:2 * n_cast + 3]

    for src_ref, dst_ref in zip(cast_in, cast_out):
        dst_ref[...] = src_ref[...].astype(BF16)

    if has_proj:
        xin_ref = refs[-1]
        y = jnp.dot(attn_ref[...], wo_ref[...], preferred_element_type=F32)
        xin_ref[...] = x_ref[...] + (1.0 + pmod_ref[2:3, :]) * y
    else:
        xin_ref = x_ref

    start, chunk, finish = _ffn_stages(xin_ref, g_ref, mod_ref, w1_ref, w3_ref, w2_ref, o_ref, h_ref, acc_ref)
    start()
    for f in range(w1_ref.shape[1] // FF_CHUNK):
        chunk(f)
    finish()


def _ffn(x, g, mod, weights, next_weights=None, proj=None):
    s, d = x.shape
    w1, w3, w2 = weights
    f = w1.shape[-1]
    steps = s // FFN_TILE
    assert f % FF_CHUNK == 0 and s % FFN_TILE == 0
    in_specs = [_row_spec(FFN_TILE, d), _resident((1, d)), _resident((3, d)),
                _resident(w1.shape), _resident(w3.shape), _resident(w2.shape)]
    args = [x, g.reshape(1, d), mod, w1, w3, w2]
    scratch = [pltpu.VMEM((FFN_TILE, d), BF16), pltpu.VMEM((FFN_TILE, d), F32)]
    if proj is not None:
        attn, pmod, w_o = proj
        in_specs += [_row_spec(FFN_TILE, d), _resident((3, d)), _resident((d, d))]
        args += [attn, pmod, w_o]
        scratch.append(pltpu.VMEM((FFN_TILE, d), F32))
    out_specs = [_row_spec(FFN_TILE, d)]
    out_shape = [jax.ShapeDtypeStruct((s, d), F32)]
    for stack, index in next_weights or ():
        rows, cols = stack.shape[-2:]
        slab = rows // steps
        assert rows % steps == 0 and slab % (2 * SUBLANES) == 0
        lead = tuple(index)
        in_specs.append(pl.BlockSpec((None,) * len(lead) + (slab, cols), lambda i, lead=lead: lead + (i, 0)))
        args.append(stack)
        out_specs.append(pl.BlockSpec((slab, cols), lambda i: (i, 0)))
        out_shape.append(jax.ShapeDtypeStruct((rows, cols), BF16))
    n_cast = len(out_specs) - 1
    out = pl.pallas_call(
        functools.partial(_ffn_kernel, n_cast=n_cast, has_proj=proj is not None),
        grid=(steps,),
        in_specs=in_specs,
        out_specs=out_specs,
        out_shape=out_shape,
        scratch_shapes=scratch,
        compiler_params=_params(("arbitrary",)),
        name="ffn",
    )(*args)
    return out[0], tuple(out[1:])


def _conv_stages(x_ref, g_ref, mod_ref, win_ref, bin_ref, dw_ref, dwb_ref, lng_ref, lnb_ref,
                 wout_ref, bout_ref, o_ref, u_ref, us_ref, v_ref):
    d = x_ref.shape[1]
    gate = 1.0 + mod_ref[2:3, :]
    base = CONV_HALO - (CONV_WIDTH - 1)

    def glu_in(k):
        rs = slice(k * CONV_SUB, (k + 1) * CONV_SUB)
        h = _modulate(x_ref[rs, :], g_ref[...], mod_ref).astype(BF16)
        uv = jnp.dot(h, win_ref[...], preferred_element_type=F32) + bin_ref[...]
        u_ref[CONV_HALO + k * CONV_SUB:CONV_HALO + (k + 1) * CONV_SUB, :] = uv[:, :d] * _sigmoid(uv[:, d:])

    def shifted(k):
        lo = k * CONV_SUB
        for b in range(1, SUBLANES):
            us_ref[b - 1] = u_ref[lo + b:lo + b + us_ref.shape[1], :]

    def depthwise(k, c):
        lo = k * CONV_SUB
        cs = slice(c * LANES, (c + 1) * LANES)
        taps = [jnp.broadcast_to(dw_ref[j:j + 1, cs], (SUBLANES, LANES)) for j in range(CONV_WIDTH)]
        bias = jnp.broadcast_to(dwb_ref[:, cs], (SUBLANES, LANES))
        for r in range(0, CONV_SUB, SUBLANES):
            acc = bias
            for j in range(CONV_WIDTH):
                a, b = divmod(base + j, SUBLANES)
                r0 = r + a * SUBLANES
                src = u_ref[lo + r0:lo + r0 + SUBLANES, cs] if b == 0 else us_ref[b - 1, r0:r0 + SUBLANES, cs]
                acc = acc + taps[j] * src
            v_ref[r:r + SUBLANES, cs] = acc

    def norm_out(k):
        rs = slice(k * CONV_SUB, (k + 1) * CONV_SUB)
        v = v_ref[...]
        mu = jnp.mean(v, axis=-1, keepdims=True)
        vc = v - mu
        var = jnp.mean(vc * vc, axis=-1, keepdims=True)
        y = vc * lax.rsqrt(var + EPS) * lng_ref[...] + lnb_ref[...]
        y = (y * _sigmoid(y)).astype(BF16)
        y = jnp.dot(y, wout_ref[...], preferred_element_type=F32) + bout_ref[...]
        o_ref[rs, :] = x_ref[rs, :] + gate * y

    return glu_in, shifted, depthwise, norm_out


def _conv_kernel(x_ref, g_ref, mod_ref, win_ref, bin_ref, dw_ref, dwb_ref, lng_ref, lnb_ref,
                 wout_ref, bout_ref, o_ref, u_ref, us_ref, v_ref):
    tm, d = x_ref.shape
    i = pl.program_id(0)

    @pl.when(i == 0)
    def _():
        u_ref[0:CONV_HALO, :] = jnp.zeros((CONV_HALO, d), F32)

    @pl.when(i > 0)
    def _():
        u_ref[0:CONV_HALO, :] = u_ref[tm:tm + CONV_HALO, :]

    glu_in, shifted, depthwise, norm_out = _conv_stages(
        x_ref, g_ref, mod_ref, win_ref, bin_ref, dw_ref, dwb_ref, lng_ref, lnb_ref, wout_ref, bout_ref,
        o_ref, u_ref, us_ref, v_ref)

    n_sub = tm // CONV_SUB
    glu_in(0)
    for k in range(n_sub):
        if k + 1 < n_sub:
            glu_in(k + 1)
        shifted(k)
        for c in range(d // LANES):
            depthwise(k, c)
        norm_out(k)


def _conv_mixer(x, g, mod, w_in, b_in, dw, dw_b, ln_g, ln_b, w_out, b_out):
    s, d = x.shape
    assert s % ROW_TILE == 0 and ROW_TILE % CONV_SUB == 0
    row = lambda v: v.reshape(1, -1)
    return pl.pallas_call(
        _conv_kernel,
        grid=(s // ROW_TILE,),
        in_specs=[
            _row_spec(ROW_TILE, d),
            _resident((1, d)),
            _resident((3, d)),
            _resident((d, 2 * d)),
            _resident((1, 2 * d)),
            _resident((CONV_WIDTH, d)),
            _resident((1, d)),
            _resident((1, d)),
            _resident((1, d)),
            _resident((d, d)),
            _resident((1, d)),
        ],
        out_specs=_row_spec(ROW_TILE, d),
        out_shape=jax.ShapeDtypeStruct((s, d), F32),
        scratch_shapes=[
            pltpu.VMEM((CONV_HALO + ROW_TILE, d), F32),
            pltpu.VMEM((SUBLANES - 1, CONV_SUB + CONV_HALO - SUBLANES, d), F32),
            pltpu.VMEM((CONV_SUB, d), F32),
        ],
        compiler_params=_params(("arbitrary",)),
        name="conv_mixer",
    )(x, row(g), mod, w_in.astype(BF16), row(b_in), dw, row(dw_b), row(ln_g), row(ln_b),
      w_out.astype(BF16), row(b_out))


def _qkv_kernel(x_ref, g_ref, mod_ref, w_ref, qg_ref, kg_ref, o_ref):
    tm, d = x_ref.shape
    h = _modulate(x_ref[...], g_ref[...], mod_ref).astype(BF16)
    qkv = jnp.dot(h, w_ref[...], preferred_element_type=F32)
    first = lax.broadcasted_iota(jnp.int32, (1, LANES), 1) < HEAD_DIM

    def head_norm(blk, gain):
        sq = blk * blk
        ms_a = jnp.sum(jnp.where(first, sq, 0.0), axis=-1, keepdims=True) * (1.0 / HEAD_DIM)
        ms_b = jnp.sum(jnp.where(first, 0.0, sq), axis=-1, keepdims=True) * (1.0 / HEAD_DIM)
        r = jnp.where(first, lax.rsqrt(ms_a + EPS), lax.rsqrt(ms_b + EPS))
        return blk * r * gain

    nb = d // LANES
    for j in range(nb):
        cs = slice(j * LANES, (j + 1) * LANES)
        o_ref[:, cs] = (head_norm(qkv[:, cs], qg_ref[...]) * (HEAD_DIM ** -0.5)).astype(BF16)
    for j in range(nb, 2 * nb):
        cs = slice(j * LANES, (j + 1) * LANES)
        o_ref[:, cs] = head_norm(qkv[:, cs], kg_ref[...]).astype(BF16)
    o_ref[:, 2 * d:] = qkv[:, 2 * d:].astype(BF16)


def _qkv_proj(x, g, mod, w_qkv, q_g, k_g):
    s, d = x.shape
    tile_gain = lambda v: jnp.tile(v, HEADS_PER_BLOCK).reshape(1, LANES)
    return pl.pallas_call(
        _qkv_kernel,
        grid=(s // ROW_TILE,),
        in_specs=[
            _row_spec(ROW_TILE, d),
            _resident((1, d)),
            _resident((3, d)),
            _resident((d, 3 * d)),
            _resident((1, LANES)),
            _resident((1, LANES)),
        ],
        out_specs=_row_spec(ROW_TILE, 3 * d),
        out_shape=jax.ShapeDtypeStruct((s, 3 * d), BF16),
        compiler_params=_params(("arbitrary",)),
        name="qkv_proj",
    )(x, g.reshape(1, d), mod, w_qkv.astype(BF16), tile_gain(q_g), tile_gain(k_g))


def _attn_kernel(q_ref, k_ref, v_ref, o_ref, c_ref, acc_ref):
    n = ATT_TILE
    s = q_ref.shape[0]
    first = lax.broadcasted_iota(jnp.int32, (1, LANES), 1) < HEAD_DIM
    row = lax.broadcasted_iota(jnp.int32, (n, n), 0)
    col = lax.broadcasted_iota(jnp.int32, (n, n), 1)
    later_and_total = jnp.concatenate([(row > col).astype(BF16), jnp.ones((n, n), BF16)], axis=1)
    strictly_causal = jnp.concatenate([col < row, col < row], axis=1)
    zero = jnp.zeros((), BF16)

    def key_blocks(jobs, c, acc):
        k2, v2 = {}, {}
        for _, _, kb, _ in jobs:
            if id(kb) not in k2:
                k0 = pl.multiple_of(kb * n, n)
                k = k_ref[pl.ds(k0, n), :]
                v = v_ref[pl.ds(k0, n), :]
                k2[id(kb)] = jnp.concatenate([jnp.where(first, k, zero), jnp.where(first, zero, k)], axis=0)
                v2[id(kb)] = jnp.concatenate([jnp.where(first, v, zero), jnp.where(first, zero, v)], axis=0)
        z = [lax.dot_general(q, k2[id(kb)], (((1,), (1,)), ((), ())), preferred_element_type=F32)
             for _, q, kb, _ in jobs]
        sp = [jnp.maximum(zi, 0.0) + jnp.log(1.0 + jnp.exp(-jnp.abs(zi))) for zi in z]
        spb = [(jnp.where(strictly_causal, si, 0.0) if job[3] else si).astype(BF16) for si, job in zip(sp, jobs)]
        run_a = [jnp.dot(si[:, :n], later_and_total, preferred_element_type=F32) for si in spb]
        run_b = [jnp.dot(si[:, n:], later_and_total, preferred_element_type=F32) for si in spb]
        a = []
        for i, (u, _, _, diagonal) in enumerate(jobs):
            suffix = jnp.concatenate([run_a[i][:, :n], run_b[i][:, :n]], axis=1) + c[u]
            c[u] = c[u] + jnp.concatenate([run_a[i][:, n:], run_b[i][:, n:]], axis=1)
            ai = jnp.exp(z[i] - sp[i] - suffix)
            if diagonal:
                ai = jnp.where(strictly_causal, ai, 0.0)
            a.append(ai.astype(BF16))
        for i, (u, _, kb, _) in enumerate(jobs):
            acc[u] = acc[u] + jnp.dot(a[i], v2[id(kb)], preferred_element_type=F32)

    def key_block(q, kb, diagonal, c, acc):
        c, acc = [c], [acc]
        key_blocks([(0, q, kb, diagonal)], c, acc)
        return c[0], acc[0]

    def query_tiles(base, n_tiles, n_fast):
        blocks = {off: base + off for off in range(-n_fast, n_tiles)}
        jobs = []
        for u in range(n_tiles):
            q = q_ref[pl.ds(pl.multiple_of(blocks[u] * n, n), n), :]
            jobs += [(u, q, blocks[u - j], j == 0) for j in range(n_fast + 1)]
        c = [jnp.zeros((n, 2 * n), F32)] * n_tiles
        acc = [jnp.zeros((n, LANES), F32)] * n_tiles
        key_blocks(jobs, c, acc)
        c_all = None
        for u in range(n_tiles):
            c_ref[u] = c[u]
            acc_ref[u] = acc[u]
            c_all = c[u] if c_all is None else jnp.minimum(c_all, c[u])

        @pl.when(jnp.min(c_all) < ATT_ZERO_WEIGHT)
        def _():
            for u in range(n_tiles):
                qb = base + u
                q = q_ref[pl.ds(pl.multiple_of(qb * n, n), n), :]

                def more(carry):
                    kb, c_min = carry
                    return jnp.logical_and(kb >= 0, c_min < ATT_ZERO_WEIGHT)

                def step(carry, u=u, q=q):
                    kb, _ = carry
                    c, acc = key_block(q, kb, False, c_ref[u], acc_ref[u])
                    c_ref[u] = c
                    acc_ref[u] = acc
                    return kb - 1, jnp.min(c)

                lax.while_loop(more, step, (qb - n_fast - 1, jnp.min(c_ref[u])))

        for u in range(n_tiles):
            q0 = pl.multiple_of((base + u) * n, n)
            o_ref[pl.ds(q0, n), :] = acc_ref[u].astype(o_ref.dtype)

    def head_tile(qb, carry):
        query_tiles(qb, 1, 0)
        return carry

    def body_tiles(g, carry):
        query_tiles(n_head + g * ATT_UNROLL, ATT_UNROLL, ATT_FAST_BLOCKS)
        return carry

    n_total = s // n
    n_head = ATT_FAST_BLOCKS + (n_total - ATT_FAST_BLOCKS) % ATT_UNROLL
    lax.fori_loop(0, n_head, head_tile, 0)
    lax.fori_loop(0, (n_total - n_head) // ATT_UNROLL, body_tiles, 0)


def _attention(qkv, d):
    s = qkv.shape[0]
    nb = d // LANES
    return pl.pallas_call(
        _attn_kernel,
        grid=(nb,),
        in_specs=[
            pl.BlockSpec((s, LANES), lambda hb: (0, hb)),
            pl.BlockSpec((s, LANES), lambda hb: (0, nb + hb)),
            pl.BlockSpec((s, LANES), lambda hb: (0, 2 * nb + hb)),
        ],
        out_specs=pl.BlockSpec((s, LANES), lambda hb: (0, hb)),
        out_shape=jax.ShapeDtypeStruct((s, d), BF16),
        scratch_shapes=[pltpu.VMEM((ATT_UNROLL, ATT_TILE, 2 * ATT_TILE), F32),
                        pltpu.VMEM((ATT_UNROLL, ATT_TILE, LANES), F32)],
        compiler_params=_params(("arbitrary",)),
        name="stickbreak_attn",
    )(qkv, qkv, qkv)


def _pool_kernel(x_ref, g_ref, mod_ref, pw_ref, pb_ref, ps_ref, o_ref, h_ref, lvl_ref):
    tm, d = x_ref.shape
    n_g = len(POOL_WINDOWS)
    gd = d // n_g
    i = pl.program_id(0)

    @pl.when(i == 0)
    def _():
        h_ref[0:POOL_HALO, :] = jnp.zeros((POOL_HALO, d), F32)

    @pl.when(i > 0)
    def _():
        h_ref[0:POOL_HALO, :] = h_ref[tm:tm + POOL_HALO, :]

    x = x_ref[...]
    h = _modulate(x, g_ref[...], mod_ref)
    h_ref[POOL_HALO:POOL_HALO + tm, :] = h
    t_pos = i * tm + lax.broadcasted_iota(jnp.int32, (tm, 1), 0)
    gate = 1.0 + mod_ref[2:3, :]

    rows = POOL_HALO + tm
    for m in range(n_g):
        back = 2 ** m
        first = SUBLANES * (m + 1)
        lanes = slice(m * gd, d)
        prev_ref = h_ref if m == 0 else lvl_ref.at[m - 1]
        level = prev_ref[first:rows, lanes] + prev_ref[first - back:rows - back, lanes]
        if m + 1 < n_g:
            lvl_ref[m, first:rows, lanes] = level
        cs = slice(m * gd, (m + 1) * gd)
        win = level[POOL_HALO - first:, 0:gd]
        cnt = jnp.minimum(t_pos + 1, POOL_WINDOWS[m]).astype(F32)
        diff = win / cnt - h[:, cs]
        y = jnp.dot(diff.astype(BF16), pw_ref[m], preferred_element_type=F32) + pb_ref[m:m + 1, :]
        o_ref[:, cs] = x[:, cs] + gate[:, cs] * (y * ps_ref[:, cs])


def _pool_mixer(x, g, mod, p_w, p_b, p_scale):
    s, d = x.shape
    ng, gd, _ = p_w.shape
    assert POOL_WINDOWS == tuple(2 ** (m + 1) for m in range(ng)) and POOL_HALO >= SUBLANES * ng
    return pl.pallas_call(
        _pool_kernel,
        grid=(s // ROW_TILE,),
        in_specs=[
            _row_spec(ROW_TILE, d),
            _resident((1, d)),
            _resident((3, d)),
            _resident((ng, gd, gd)),
            _resident((ng, gd)),
            _resident((1, d)),
        ],
        out_specs=_row_spec(ROW_TILE, d),
        out_shape=jax.ShapeDtypeStruct((s, d), F32),
        scratch_shapes=[pltpu.VMEM((POOL_HALO + ROW_TILE, d), F32),
                        pltpu.VMEM((ng - 1, POOL_HALO + ROW_TILE, d), F32)],
        compiler_params=_params(("arbitrary",)),
        name="pool_mixer",
    )(x, g.reshape(1, d), mod, p_w.astype(BF16), p_b, p_scale.reshape(1, d))


def kernel(x, c, cond_w, cond_b, ada_w, ada_b, norm_g, ffn_w1, ffn_w3, ffn_w2, a_w_in, a_b_in, a_dw, a_dw_b, a_ln_g, a_ln_b, a_w_out, a_b_out, b_w_qkv, b_q_g, b_k_g, b_w_o, p_w, p_b, p_scale):
    batch, s, d = x.shape
    assert batch == 1 and d == N_HEADS * HEAD_DIM
    assert s % ROW_TILE == 0 and s % ATT_TILE == 0 and s // ATT_TILE >= ATT_FAST_BLOCKS + ATT_UNROLL
    depth = ada_w.shape[0]
    mod = _conditioning(c, cond_w, cond_b, ada_w, ada_b)
    xs = x.reshape(s, d)
    stacks = (ffn_w1, ffn_w3, ffn_w2)
    weights = tuple(w[0, 0].astype(BF16) for w in stacks)
    ia = ib = ic = 0
    for i in range(depth):
        xs, weights = _ffn(xs, norm_g[i, 0], mod[i, 0], weights, [(w, (i, 1)) for w in stacks])
        kind = i % 3
        proj = None
        if kind == 0:
            xs = _conv_mixer(xs, norm_g[i, 1], mod[i, 1], a_w_in[ia], a_b_in[ia], a_dw[ia], a_dw_b[ia],
                             a_ln_g[ia], a_ln_b[ia], a_w_out[ia], a_b_out[ia])
            ia += 1
        elif kind == 1:
            qkv = _qkv_proj(xs, norm_g[i, 1], mod[i, 1], b_w_qkv[ib], b_q_g[ib], b_k_g[ib])
            proj = (_attention(qkv, d), mod[i, 1], b_w_o[ib].astype(BF16))
            ib += 1
        else:
            xs = _pool_mixer(xs, norm_g[i, 1], mod[i, 1], p_w[ic], p_b[ic], p_scale[ic])
            ic += 1
        following = [(w, (i + 1, 0)) for w in stacks] if i + 1 < depth else None
        xs, weights = _ffn(xs, norm_g[i, 2], mod[i, 2], weights, following, proj)
    return xs.reshape(batch, s, d)
```

```python
import functools

import jax
import jax.numpy as jnp
from jax import lax
from jax.experimental import pallas as pl
from jax.experimental.pallas import tpu as pltpu

F32 = jnp.float32
BF16 = jnp.bfloat16

EPS = 1e-6
N_HEADS = 16
HEAD_DIM = 64
CONV_WIDTH = 31
POOL_WINDOWS = (2, 4, 8, 16)
LANES = 128
SUBLANES = 8
HEADS_PER_BLOCK = LANES // HEAD_DIM

ROW_TILE = 512
FFN_TILE = 1024
COND_COLS = 1152
FF_CHUNK = 256
CONV_HALO = 32
CONV_SUB = 256
POOL_HALO = 32
ATT_TILE = 128
ATT_FAST_BLOCKS = 2
ATT_UNROLL = 14
ATT_ZERO_WEIGHT = 104.0
VMEM_LIMIT = 56 * 1024 * 1024


def _sigmoid(x):
    return 1.0 / (1.0 + jnp.exp(-x))


def _modulate(x, g, mod_ref):
    ms = jnp.mean(x * x, axis=-1, keepdims=True)
    gs = g * (1.0 + mod_ref[1:2, :])
    return (x * lax.rsqrt(ms + EPS)) * gs + mod_ref[0:1, :]


def _resident(shape):
    zeros = (0,) * len(shape)
    return pl.BlockSpec(shape, lambda *_: zeros, pipeline_mode=pl.Buffered(1))


def _row_spec(tile, width):
    return pl.BlockSpec((tile, width), lambda i: (i, 0))


def _params(semantics):
    return pltpu.CompilerParams(dimension_semantics=semantics, vmem_limit_bytes=VMEM_LIMIT)


def _cond_embed_kernel(c_ref, w_ref, b_ref, e_ref):
    t = jnp.sum(c_ref[...] * w_ref[...], axis=0, keepdims=True) + b_ref[...]
    e_ref[...] = t * _sigmoid(t)


def _cond_mod_kernel(e_ref, w_ref, b_ref, o_ref):
    o_ref[...] = jnp.sum(e_ref[...] * w_ref[...], axis=0, keepdims=True) + b_ref[...]


def _cond_embed(c, cond_w, cond_b):
    d = cond_w.shape[0]
    e = pl.pallas_call(
        _cond_embed_kernel,
        out_shape=jax.ShapeDtypeStruct((1, d), F32),
        name="cond_embed",
    )(c.reshape(d, 1), cond_w, cond_b.reshape(1, d))
    return e.reshape(d, 1)


def _cond_mod(e_col, ada_w, ada_b, layer):
    _, d, n = ada_w.shape
    return pl.pallas_call(
        _cond_mod_kernel,
        grid=(n // COND_COLS,),
        in_specs=[
            pl.BlockSpec((d, 1), lambda j: (0, 0)),
            pl.BlockSpec((None, d, COND_COLS), lambda j: (layer, 0, j)),
            pl.BlockSpec((None, 1, COND_COLS), lambda j: (layer, 0, j)),
        ],
        out_specs=pl.BlockSpec((1, COND_COLS), lambda j: (0, j)),
        out_shape=jax.ShapeDtypeStruct((1, n), F32),
        compiler_params=_params(("arbitrary",)),
        name="cond_mod",
    )(e_col, ada_w, ada_b)


def _ffn_stages(x_ref, g_ref, mod_ref, w1_ref, w3_ref, w2_ref, o_ref, h_ref, acc_ref):
    def start():
        h_ref[...] = _modulate(x_ref[...], g_ref[...], mod_ref).astype(BF16)
        acc_ref[...] = jnp.zeros_like(acc_ref)

    def chunk(f):
        fs = slice(f * FF_CHUNK, (f + 1) * FF_CHUNK)
        h = h_ref[...]
        a = jnp.dot(h, w1_ref[:, fs], preferred_element_type=F32)
        b = jnp.dot(h, w3_ref[:, fs], preferred_element_type=F32)
        act = (a * _sigmoid(a) * b).astype(BF16)
        acc_ref[...] += jnp.dot(act, w2_ref[fs, :], preferred_element_type=F32)

    def finish():
        o_ref[...] = x_ref[...] + (0.5 * (1.0 + mod_ref[2:3, :])) * acc_ref[...]

    return start, chunk, finish


def _ffn_kernel(*refs, n_cast, has_proj, has_cond):
    refs = list(refs)
    x_ref, g_ref, mod_ref, w1_ref, w3_ref, w2_ref = refs[:6]
    del refs[:6]
    if has_proj:
        attn_ref, pmod_ref, wo_ref = refs[:3]
        del refs[:3]
    cast_in = refs[:n_cast]
    del refs[:n_cast]
    if has_cond:
        e_ref, aw_ref, ab_ref = refs[:3]
        del refs[:3]
    o_ref = refs.pop(0)
    cast_out = refs[:n_cast]
    del refs[:n_cast]
    if has_cond:
        nmod_ref = refs.pop(0)
    h_ref, acc_ref = refs[:2]

    for src_ref, dst_ref in zip(cast_in, cast_out):
        dst_ref[...] = src_ref[...].astype(BF16)
    if has_cond:
        @pl.when(pl.program_id(0) < has_cond)
        def _():
            nmod_ref[...] = jnp.sum(e_ref[...] * aw_ref[...], axis=0, keepdims=True) + ab_ref[...]

    if has_proj:
        xin_ref = refs[2]
        y = jnp.dot(attn_ref[...], wo_ref[...], preferred_element_type=F32)
        xin_ref[...] = x_ref[...] + (1.0 + pmod_ref[2:3, :]) * y
    else:
        xin_ref = x_ref

    start, chunk, finish = _ffn_stages(xin_ref, g_ref, mod_ref, w1_ref, w3_ref, w2_ref, o_ref, h_ref, acc_ref)
    start()
    for f in range(w1_ref.shape[1] // FF_CHUNK):
        chunk(f)
    finish()


def _ffn(x, g, mod, weights, next_weights=None, proj=None, cond=None):
    s, d = x.shape
    w1, w3, w2 = weights
    f = w1.shape[-1]
    steps = s // FFN_TILE
    assert f % FF_CHUNK == 0 and s % FFN_TILE == 0
    in_specs = [_row_spec(FFN_TILE, d), _resident((1, d)), _resident((3, d)),
                _resident(w1.shape), _resident(w3.shape), _resident(w2.shape)]
    args = [x, g.reshape(1, d), mod, w1, w3, w2]
    scratch = [pltpu.VMEM((FFN_TILE, d), BF16), pltpu.VMEM((FFN_TILE, d), F32)]
    if proj is not None:
        attn, pmod, w_o = proj
        in_specs += [_row_spec(FFN_TILE, d), _resident((3, d)), _resident((d, d))]
        args += [attn, pmod, w_o]
        scratch.append(pltpu.VMEM((FFN_TILE, d), F32))
    out_specs = [_row_spec(FFN_TILE, d)]
    out_shape = [jax.ShapeDtypeStruct((s, d), F32)]
    for stack, index in next_weights or ():
        rows, cols = stack.shape[-2:]
        slab = rows // steps
        assert rows % steps == 0 and slab % (2 * SUBLANES) == 0
        lead = tuple(index)
        in_specs.append(pl.BlockSpec((None,) * len(lead) + (slab, cols), lambda i, lead=lead: lead + (i, 0)))
        args.append(stack)
        out_specs.append(pl.BlockSpec((slab, cols), lambda i: (i, 0)))
        out_shape.append(jax.ShapeDtypeStruct((rows, cols), BF16))
    n_cast = len(out_specs) - 1
    if cond is not None:
        e_col, ada_w, ada_b, layer = cond
        n = ada_w.shape[-1]
        slabs = n // COND_COLS
        assert n % COND_COLS == 0 and slabs <= steps
        col = lambda i: jnp.minimum(i, slabs - 1)
        in_specs += [_resident((d, 1)),
                     pl.BlockSpec((None, d, COND_COLS), lambda i: (layer, 0, col(i))),
                     pl.BlockSpec((None, 1, COND_COLS), lambda i: (layer, 0, col(i)))]
        args += [e_col, ada_w, ada_b]
        out_specs.append(pl.BlockSpec((1, COND_COLS), lambda i: (0, col(i))))
        out_shape.append(jax.ShapeDtypeStruct((1, n), F32))
    out = pl.pallas_call(
        functools.partial(_ffn_kernel, n_cast=n_cast, has_proj=proj is not None,
                          has_cond=0 if cond is None else slabs),
        grid=(steps,),
        in_specs=in_specs,
        out_specs=out_specs,
        out_shape=out_shape,
        scratch_shapes=scratch,
        compiler_params=_params(("arbitrary",)),
        name="ffn",
    )(*args)
    return out[0], tuple(out[1:1 + n_cast]), (out[1 + n_cast] if cond is not None else None)


def _conv_stages(x_ref, g_ref, mod_ref, win_ref, bin_ref, dw_ref, dwb_ref, lng_ref, lnb_ref,
                 wout_ref, bout_ref, o_ref, u_ref, us_ref, v_ref):
    d = x_ref.shape[1]
    gate = 1.0 + mod_ref[2:3, :]
    base = CONV_HALO - (CONV_WIDTH - 1)

    def glu_in(k):
        rs = slice(k * CONV_SUB, (k + 1) * CONV_SUB)
        h = _modulate(x_ref[rs, :], g_ref[...], mod_ref).astype(BF16)
        uv = jnp.dot(h, win_ref[...], preferred_element_type=F32) + bin_ref[...]
        u_ref[CONV_HALO + k * CONV_SUB:CONV_HALO + (k + 1) * CONV_SUB, :] = uv[:, :d] * _sigmoid(uv[:, d:])

    def shifted(k):
        lo = k * CONV_SUB
        for b in range(1, SUBLANES):
            us_ref[b - 1] = u_ref[lo + b:lo + b + us_ref.shape[1], :]

    def depthwise(k, c):
        lo = k * CONV_SUB
        cs = slice(c * LANES, (c + 1) * LANES)
        taps = [jnp.broadcast_to(dw_ref[j:j + 1, cs], (SUBLANES, LANES)) for j in range(CONV_WIDTH)]
        bias = jnp.broadcast_to(dwb_ref[:, cs], (SUBLANES, LANES))
        for r in range(0, CONV_SUB, SUBLANES):
            acc = bias
            for j in range(CONV_WIDTH):
                a, b = divmod(base + j, SUBLANES)
                r0 = r + a * SUBLANES
                src = u_ref[lo + r0:lo + r0 + SUBLANES, cs] if b == 0 else us_ref[b - 1, r0:r0 + SUBLANES, cs]
                acc = acc + taps[j] * src
            v_ref[r:r + SUBLANES, cs] = acc

    def norm_out(k):
        rs = slice(k * CONV_SUB, (k + 1) * CONV_SUB)
        v = v_ref[...]
        mu = jnp.mean(v, axis=-1, keepdims=True)
        vc = v - mu
        var = jnp.mean(vc * vc, axis=-1, keepdims=True)
        y = vc * lax.rsqrt(var + EPS) * lng_ref[...] + lnb_ref[...]
        y = (y * _sigmoid(y)).astype(BF16)
        y = jnp.dot(y, wout_ref[...], preferred_element_type=F32) + bout_ref[...]
        o_ref[rs, :] = x_ref[rs, :] + gate * y

    return glu_in, shifted, depthwise, norm_out


def _conv_kernel(x_ref, g_ref, mod_ref, win_ref, bin_ref, dw_ref, dwb_ref, lng_ref, lnb_ref,
                 wout_ref, bout_ref, o_ref, u_ref, us_ref, v_ref):
    tm, d = x_ref.shape
    i = pl.program_id(0)

    @pl.when(i == 0)
    def _():
        u_ref[0:CONV_HALO, :] = jnp.zeros((CONV_HALO, d), F32)

    @pl.when(i > 0)
    def _():
        u_ref[0:CONV_HALO, :] = u_ref[tm:tm + CONV_HALO, :]

    glu_in, shifted, depthwise, norm_out = _conv_stages(
        x_ref, g_ref, mod_ref, win_ref, bin_ref, dw_ref, dwb_ref, lng_ref, lnb_ref, wout_ref, bout_ref,
        o_ref, u_ref, us_ref, v_ref)

    n_sub = tm // CONV_SUB
    glu_in(0)
    for k in range(n_sub):
        if k + 1 < n_sub:
            glu_in(k + 1)
        shifted(k)
        for c in range(d // LANES):
            depthwise(k, c)
        norm_out(k)


def _conv_mixer(x, g, mod, w_in, b_in, dw, dw_b, ln_g, ln_b, w_out, b_out):
    s, d = x.shape
    assert s % ROW_TILE == 0 and ROW_TILE % CONV_SUB == 0
    row = lambda v: v.reshape(1, -1)
    return pl.pallas_call(
        _conv_kernel,
        grid=(s // ROW_TILE,),
        in_specs=[
            _row_spec(ROW_TILE, d),
            _resident((1, d)),
            _resident((3, d)),
            _resident((d, 2 * d)),
            _resident((1, 2 * d)),
            _resident((CONV_WIDTH, d)),
            _resident((1, d)),
            _resident((1, d)),
            _resident((1, d)),
            _resident((d, d)),
            _resident((1, d)),
        ],
        out_specs=_row_spec(ROW_TILE, d),
        out_shape=jax.ShapeDtypeStruct((s, d), F32),
        scratch_shapes=[
            pltpu.VMEM((CONV_HALO + ROW_TILE, d), F32),
            pltpu.VMEM((SUBLANES - 1, CONV_SUB + CONV_HALO - SUBLANES, d), F32),
            pltpu.VMEM((CONV_SUB, d), F32),
        ],
        compiler_params=_params(("arbitrary",)),
        name="conv_mixer",
    )(x, row(g), mod, w_in.astype(BF16), row(b_in), dw, row(dw_b), row(ln_g), row(ln_b),
      w_out.astype(BF16), row(b_out))


def _qkv_kernel(x_ref, g_ref, mod_ref, w_ref, qg_ref, kg_ref, o_ref):
    tm, d = x_ref.shape
    h = _modulate(x_ref[...], g_ref[...], mod_ref).astype(BF16)
    qkv = jnp.dot(h, w_ref[...], preferred_element_type=F32)
    first = lax.broadcasted_iota(jnp.int32, (1, LANES), 1) < HEAD_DIM

    def head_norm(blk, gain):
        sq = blk * blk
        ms_a = jnp.sum(jnp.where(first, sq, 0.0), axis=-1, keepdims=True) * (1.0 / HEAD_DIM)
        ms_b = jnp.sum(jnp.where(first, 0.0, sq), axis=-1, keepdims=True) * (1.0 / HEAD_DIM)
        r = jnp.where(first, lax.rsqrt(ms_a + EPS), lax.rsqrt(ms_b + EPS))
        return blk * r * gain

    nb = d // LANES
    for j in range(nb):
        cs = slice(j * LANES, (j + 1) * LANES)
        o_ref[:, cs] = (head_norm(qkv[:, cs], qg_ref[...]) * (HEAD_DIM ** -0.5)).astype(BF16)
    for j in range(nb, 2 * nb):
        cs = slice(j * LANES, (j + 1) * LANES)
        o_ref[:, cs] = head_norm(qkv[:, cs], kg_ref[...]).astype(BF16)
    o_ref[:, 2 * d:] = qkv[:, 2 * d:].astype(BF16)


def _qkv_proj(x, g, mod, w_qkv, q_g, k_g):
    s, d = x.shape
    tile_gain = lambda v: jnp.tile(v, HEADS_PER_BLOCK).reshape(1, LANES)
    return pl.pallas_call(
        _qkv_kernel,
        grid=(s // ROW_TILE,),
        in_specs=[
            _row_spec(ROW_TILE, d),
            _resident((1, d)),
            _resident((3, d)),
            _resident((d, 3 * d)),
            _resident((1, LANES)),
            _resident((1, LANES)),
        ],
        out_specs=_row_spec(ROW_TILE, 3 * d),
        out_shape=jax.ShapeDtypeStruct((s, 3 * d), BF16),
        compiler_params=_params(("arbitrary",)),
        name="qkv_proj",
    )(x, g.reshape(1, d), mod, w_qkv.astype(BF16), tile_gain(q_g), tile_gain(k_g))


def _attn_kernel(q_ref, k_ref, v_ref, o_ref, c_ref, acc_ref):
    n = ATT_TILE
    s = q_ref.shape[0]
    first = lax.broadcasted_iota(jnp.int32, (1, LANES), 1) < HEAD_DIM
    row = lax.broadcasted_iota(jnp.int32, (n, n), 0)
    col = lax.broadcasted_iota(jnp.int32, (n, n), 1)
    later_and_total = jnp.concatenate([(row > col).astype(BF16), jnp.ones((n, n), BF16)], axis=1)
    strictly_causal = jnp.concatenate([col < row, col < row], axis=1)
    zero = jnp.zeros((), BF16)

    def key_blocks(jobs, c, acc):
        k2, v2 = {}, {}
        for _, _, kb, _ in jobs:
            if id(kb) not in k2:
                k0 = pl.multiple_of(kb * n, n)
                k = k_ref[pl.ds(k0, n), :]
                v = v_ref[pl.ds(k0, n), :]
                k2[id(kb)] = jnp.concatenate([jnp.where(first, k, zero), jnp.where(first, zero, k)], axis=0)
                v2[id(kb)] = jnp.concatenate([jnp.where(first, v, zero), jnp.where(first, zero, v)], axis=0)
        z = [lax.dot_general(q, k2[id(kb)], (((1,), (1,)), ((), ())), preferred_element_type=F32)
             for _, q, kb, _ in jobs]
        sp = [jnp.maximum(zi, 0.0) + jnp.log(1.0 + jnp.exp(-jnp.abs(zi))) for zi in z]
        spb = [(jnp.where(strictly_causal, si, 0.0) if job[3] else si).astype(BF16) for si, job in zip(sp, jobs)]
        run_a = [jnp.dot(si[:, :n], later_and_total, preferred_element_type=F32) for si in spb]
        run_b = [jnp.dot(si[:, n:], later_and_total, preferred_element_type=F32) for si in spb]
        a = []
        for i, (u, _, _, diagonal) in enumerate(jobs):
            suffix = jnp.concatenate([run_a[i][:, :n], run_b[i][:, :n]], axis=1) + c[u]
            c[u] = c[u] + jnp.concatenate([run_a[i][:, n:], run_b[i][:, n:]], axis=1)
            ai = jnp.exp(z[i] - sp[i] - suffix)
            if diagonal:
                ai = jnp.where(strictly_causal, ai, 0.0)
            a.append(ai.astype(BF16))
        for i, (u, _, kb, _) in enumerate(jobs):
            acc[u] = acc[u] + jnp.dot(a[i], v2[id(kb)], preferred_element_type=F32)

    def key_block(q, kb, diagonal, c, acc):
        c, acc = [c], [acc]
        key_blocks([(0, q, kb, diagonal)], c, acc)
        return c[0], acc[0]

    def query_tiles(base, n_tiles, n_fast):
        blocks = {off: base + off for off in range(-n_fast, n_tiles)}
        jobs = []
        for u in range(n_tiles):
            q = q_ref[pl.ds(pl.multiple_of(blocks[u] * n, n), n), :]
            jobs += [(u, q, blocks[u - j], j == 0) for j in range(n_fast + 1)]
        c = [jnp.zeros((n, 2 * n), F32)] * n_tiles
        acc = [jnp.zeros((n, LANES), F32)] * n_tiles
        key_blocks(jobs, c, acc)
        c_all = None
        for u in range(n_tiles):
            c_ref[u] = c[u]
            acc_ref[u] = acc[u]
            c_all = c[u] if c_all is None else jnp.minimum(c_all, c[u])

        @pl.when(jnp.min(c_all) < ATT_ZERO_WEIGHT)
        def _():
            for u in range(n_tiles):
                qb = base + u
                q = q_ref[pl.ds(pl.multiple_of(qb * n, n), n), :]

                def more(carry):
                    kb, c_min = carry
                    return jnp.logical_and(kb >= 0, c_min < ATT_ZERO_WEIGHT)

                def step(carry, u=u, q=q):
                    kb, _ = carry
                    c, acc = key_block(q, kb, False, c_ref[u], acc_ref[u])
                    c_ref[u] = c
                    acc_ref[u] = acc
                    return kb - 1, jnp.min(c)

                lax.while_loop(more, step, (qb - n_fast - 1, jnp.min(c_ref[u])))

        for u in range(n_tiles):
            q0 = pl.multiple_of((base + u) * n, n)
            o_ref[pl.ds(q0, n), :] = acc_ref[u].astype(o_ref.dtype)

    def head_tile(qb, carry):
        query_tiles(qb, 1, 0)
        return carry

    def body_tiles(g, carry):
        query_tiles(n_head + g * ATT_UNROLL, ATT_UNROLL, ATT_FAST_BLOCKS)
        return carry

    n_total = s // n
    n_head = ATT_FAST_BLOCKS + (n_total - ATT_FAST_BLOCKS) % ATT_UNROLL
    lax.fori_loop(0, n_head, head_tile, 0)
    lax.fori_loop(0, (n_total - n_head) // ATT_UNROLL, body_tiles, 0)


def _attention(qkv, d):
    s = qkv.shape[0]
    nb = d // LANES
    return pl.pallas_call(
        _attn_kernel,
        grid=(nb,),
        in_specs=[
            pl.BlockSpec((s, LANES), lambda hb: (0, hb)),
            pl.BlockSpec((s, LANES), lambda hb: (0, nb + hb)),
            pl.BlockSpec((s, LANES), lambda hb: (0, 2 * nb + hb)),
        ],
        out_specs=pl.BlockSpec((s, LANES), lambda hb: (0, hb)),
        out_shape=jax.ShapeDtypeStruct((s, d), BF16),
        scratch_shapes=[pltpu.VMEM((ATT_UNROLL, ATT_TILE, 2 * ATT_TILE), F32),
                        pltpu.VMEM((ATT_UNROLL, ATT_TILE, LANES), F32)],
        compiler_params=_params(("arbitrary",)),
        name="stickbreak_attn",
    )(qkv, qkv, qkv)


def _pool_kernel(x_ref, g_ref, mod_ref, pw_ref, pb_ref, ps_ref, o_ref, h_ref, lvl_ref):
    tm, d = x_ref.shape
    n_g = len(POOL_WINDOWS)
    gd = d // n_g
    i = pl.program_id(0)

    @pl.when(i == 0)
    def _():
        h_ref[0:POOL_HALO, :] = jnp.zeros((POOL_HALO, d), F32)

    @pl.when(i > 0)
    def _():
        h_ref[0:POOL_HALO, :] = h_ref[tm:tm + POOL_HALO, :]

    x = x_ref[...]
    h = _modulate(x, g_ref[...], mod_ref)
    h_ref[POOL_HALO:POOL_HALO + tm, :] = h
    t_pos = i * tm + lax.broadcasted_iota(jnp.int32, (tm, 1), 0)
    gate = 1.0 + mod_ref[2:3, :]

    rows = POOL_HALO + tm
    for m in range(n_g):
        back = 2 ** m
        first = SUBLANES * (m + 1)
        lanes = slice(m * gd, d)
        prev_ref = h_ref if m == 0 else lvl_ref.at[m - 1]
        level = prev_ref[first:rows, lanes] + prev_ref[first - back:rows - back, lanes]
        if m + 1 < n_g:
            lvl_ref[m, first:rows, lanes] = level
        cs = slice(m * gd, (m + 1) * gd)
        win = level[POOL_HALO - first:, 0:gd]
        cnt = jnp.minimum(t_pos + 1, POOL_WINDOWS[m]).astype(F32)
        diff = win / cnt - h[:, cs]
        y = jnp.dot(diff.astype(BF16), pw_ref[m], preferred_element_type=F32) + pb_ref[m:m + 1, :]
        o_ref[:, cs] = x[:, cs] + gate[:, cs] * (y * ps_ref[:, cs])


def _pool_mixer(x, g, mod, p_w, p_b, p_scale):
    s, d = x.shape
    ng, gd, _ = p_w.shape
    assert POOL_WINDOWS == tuple(2 ** (m + 1) for m in range(ng)) and POOL_HALO >= SUBLANES * ng
    return pl.pallas_call(
        _pool_kernel,
        grid=(s // ROW_TILE,),
        in_specs=[
            _row_spec(ROW_TILE, d),
            _resident((1, d)),
            _resident((3, d)),
            _resident((ng, gd, gd)),
            _resident((ng, gd)),
            _resident((1, d)),
        ],
        out_specs=_row_spec(ROW_TILE, d),
        out_shape=jax.ShapeDtypeStruct((s, d), F32),
        scratch_shapes=[pltpu.VMEM((POOL_HALO + ROW_TILE, d), F32),
                        pltpu.VMEM((ng - 1, POOL_HALO + ROW_TILE, d), F32)],
        compiler_params=_params(("arbitrary",)),
        name="pool_mixer",
    )(x, g.reshape(1, d), mod, p_w.astype(BF16), p_b, p_scale.reshape(1, d))


def kernel(x, c, cond_w, cond_b, ada_w, ada_b, norm_g, ffn_w1, ffn_w3, ffn_w2, a_w_in, a_b_in, a_dw, a_dw_b, a_ln_g, a_ln_b, a_w_out, a_b_out, b_w_qkv, b_q_g, b_k_g, b_w_o, p_w, p_b, p_scale):
    batch, s, d = x.shape
    assert batch == 1 and d == N_HEADS * HEAD_DIM
    assert s % ROW_TILE == 0 and s % ATT_TILE == 0 and s // ATT_TILE >= ATT_FAST_BLOCKS + ATT_UNROLL
    depth = ada_w.shape[0]
    n_mod = ada_w.shape[-1]
    e_col = _cond_embed(c, cond_w, cond_b)
    ada_b3 = ada_b.reshape(depth, 1, n_mod)
    as_mod = lambda row: row.reshape(3, 3, d)
    mod = as_mod(_cond_mod(e_col, ada_w, ada_b3, 0))
    xs = x.reshape(s, d)
    stacks = (ffn_w1, ffn_w3, ffn_w2)
    weights = tuple(w[0, 0].astype(BF16) for w in stacks)
    ia = ib = ic = 0
    for i in range(depth):
        cond = (e_col, ada_w, ada_b3, i + 1) if i + 1 < depth else None
        xs, weights, next_mod = _ffn(xs, norm_g[i, 0], mod[0], weights, [(w, (i, 1)) for w in stacks], cond=cond)
        kind = i % 3
        proj = None
        if kind == 0:
            xs = _conv_mixer(xs, norm_g[i, 1], mod[1], a_w_in[ia], a_b_in[ia], a_dw[ia], a_dw_b[ia],
                             a_ln_g[ia], a_ln_b[ia], a_w_out[ia], a_b_out[ia])
            ia += 1
        elif kind == 1:
            qkv = _qkv_proj(xs, norm_g[i, 1], mod[1], b_w_qkv[ib], b_q_g[ib], b_k_g[ib])
            proj = (_attention(qkv, d), mod[1], b_w_o[ib].astype(BF16))
            ib += 1
        else:
            xs = _pool_mixer(xs, norm_g[i, 1], mod[1], p_w[ic], p_b[ic], p_scale[ic])
            ic += 1
        following = [(w, (i + 1, 0)) for w in stacks] if i + 1 < depth else None
        xs, weights, _ = _ffn(xs, norm_g[i, 2], mod[2], weights, following, proj)
        if next_mod is not None:
            mod = as_mod(next_mod)
    return xs.reshape(batch, s, d)
```

```python
import functools

import jax
import jax.numpy as jnp
from jax import lax
from jax.experimental import pallas as pl
from jax.experimental.pallas import tpu as pltpu

F32 = jnp.float32
BF16 = jnp.bfloat16

EPS = 1e-6
N_HEADS = 16
HEAD_DIM = 64
CONV_WIDTH = 31
POOL_WINDOWS = (2, 4, 8, 16)
LANES = 128
SUBLANES = 8
HEADS_PER_BLOCK = LANES // HEAD_DIM

ROW_TILE = 512
FFN_TILE = 1024
COND_COLS = 1152
FF_CHUNK = 256
CONV_HALO = 32
CONV_SUB = 256
POOL_HALO = 32
ATT_TILE = 128
ATT_FAST_BLOCKS = 2
ATT_UNROLL = 14
LOG2_E = 1.4426950408889634
ATT_ZERO_WEIGHT = 104.0 * LOG2_E
VMEM_LIMIT = 56 * 1024 * 1024


def _sigmoid(x):
    return 1.0 / (1.0 + jnp.exp(-x))


def _modulate(x, g, mod_ref):
    ms = jnp.mean(x * x, axis=-1, keepdims=True)
    gs = g * (1.0 + mod_ref[1:2, :])
    return (x * lax.rsqrt(ms + EPS)) * gs + mod_ref[0:1, :]


def _resident(shape):
    zeros = (0,) * len(shape)
    return pl.BlockSpec(shape, lambda *_: zeros, pipeline_mode=pl.Buffered(1))


def _row_spec(tile, width):
    return pl.BlockSpec((tile, width), lambda i: (i, 0))


def _params(semantics):
    return pltpu.CompilerParams(dimension_semantics=semantics, vmem_limit_bytes=VMEM_LIMIT)


def _cond_embed_kernel(c_ref, w_ref, b_ref, e_ref):
    t = jnp.sum(c_ref[...] * w_ref[...], axis=0, keepdims=True) + b_ref[...]
    e_ref[...] = t * _sigmoid(t)


def _cond_mod_kernel(e_ref, w_ref, b_ref, o_ref):
    o_ref[...] = jnp.sum(e_ref[...] * w_ref[...], axis=0, keepdims=True) + b_ref[...]


def _cond_embed(c, cond_w, cond_b):
    d = cond_w.shape[0]
    e = pl.pallas_call(
        _cond_embed_kernel,
        out_shape=jax.ShapeDtypeStruct((1, d), F32),
        name="cond_embed",
    )(c.reshape(d, 1), cond_w, cond_b.reshape(1, d))
    return e.reshape(d, 1)


def _cond_mod(e_col, ada_w, ada_b, layer):
    _, d, n = ada_w.shape
    return pl.pallas_call(
        _cond_mod_kernel,
        grid=(n // COND_COLS,),
        in_specs=[
            pl.BlockSpec((d, 1), lambda j: (0, 0)),
            pl.BlockSpec((None, d, COND_COLS), lambda j: (layer, 0, j)),
            pl.BlockSpec((None, 1, COND_COLS), lambda j: (layer, 0, j)),
        ],
        out_specs=pl.BlockSpec((1, COND_COLS), lambda j: (0, j)),
        out_shape=jax.ShapeDtypeStruct((1, n), F32),
        compiler_params=_params(("arbitrary",)),
        name="cond_mod",
    )(e_col, ada_w, ada_b)


def _ffn_stages(x_ref, g_ref, mod_ref, w1_ref, w3_ref, w2_ref, o_ref, h_ref, acc_ref):
    def start():
        h_ref[...] = _modulate(x_ref[...], g_ref[...], mod_ref).astype(BF16)
        acc_ref[...] = jnp.zeros_like(acc_ref)

    def chunk(f):
        fs = slice(f * FF_CHUNK, (f + 1) * FF_CHUNK)
        h = h_ref[...]
        a = jnp.dot(h, w1_ref[:, fs], preferred_element_type=F32)
        b = jnp.dot(h, w3_ref[:, fs], preferred_element_type=F32)
        act = (a * _sigmoid(a) * b).astype(BF16)
        acc_ref[...] += jnp.dot(act, w2_ref[fs, :], preferred_element_type=F32)

    def finish():
        o_ref[...] = x_ref[...] + (0.5 * (1.0 + mod_ref[2:3, :])) * acc_ref[...]

    return start, chunk, finish


def _ffn_kernel(*refs, n_cast, has_proj, has_cond):
    refs = list(refs)
    x_ref, g_ref, mod_ref, w1_ref, w3_ref, w2_ref = refs[:6]
    del refs[:6]
    if has_proj:
        attn_ref, pmod_ref, wo_ref = refs[:3]
        del refs[:3]
    cast_in = refs[:n_cast]
    del refs[:n_cast]
    if has_cond:
        e_ref, aw_ref, ab_ref = refs[:3]
        del refs[:3]
    o_ref = refs.pop(0)
    cast_out = refs[:n_cast]
    del refs[:n_cast]
    if has_cond:
        nmod_ref = refs.pop(0)
    h_ref, acc_ref = refs[:2]

    for src_ref, dst_ref in zip(cast_in, cast_out):
        dst_ref[...] = src_ref[...].astype(BF16)
    if has_cond:
        @pl.when(pl.program_id(0) < has_cond)
        def _():
            nmod_ref[...] = jnp.sum(e_ref[...] * aw_ref[...], axis=0, keepdims=True) + ab_ref[...]

    if has_proj:
        xin_ref = refs[2]
        y = jnp.dot(attn_ref[...], wo_ref[...], preferred_element_type=F32)
        xin_ref[...] = x_ref[...] + (1.0 + pmod_ref[2:3, :]) * y
    else:
        xin_ref = x_ref

    start, chunk, finish = _ffn_stages(xin_ref, g_ref, mod_ref, w1_ref, w3_ref, w2_ref, o_ref, h_ref, acc_ref)
    start()
    for f in range(w1_ref.shape[1] // FF_CHUNK):
        chunk(f)
    finish()


def _ffn(x, g, mod, weights, next_weights=None, proj=None, cond=None):
    s, d = x.shape
    w1, w3, w2 = weights
    f = w1.shape[-1]
    steps = s // FFN_TILE
    assert f % FF_CHUNK == 0 and s % FFN_TILE == 0
    in_specs = [_row_spec(FFN_TILE, d), _resident((1, d)), _resident((3, d)),
                _resident(w1.shape), _resident(w3.shape), _resident(w2.shape)]
    args = [x, g.reshape(1, d), mod, w1, w3, w2]
    scratch = [pltpu.VMEM((FFN_TILE, d), BF16), pltpu.VMEM((FFN_TILE, d), F32)]
    if proj is not None:
        attn, pmod, w_o = proj
        in_specs += [_row_spec(FFN_TILE, d), _resident((3, d)), _resident((d, d))]
        args += [attn, pmod, w_o]
        scratch.append(pltpu.VMEM((FFN_TILE, d), F32))
    out_specs = [_row_spec(FFN_TILE, d)]
    out_shape = [jax.ShapeDtypeStruct((s, d), F32)]
    for stack, index in next_weights or ():
        rows, cols = stack.shape[-2:]
        slab = rows // steps
        assert rows % steps == 0 and slab % (2 * SUBLANES) == 0
        lead = tuple(index)
        in_specs.append(pl.BlockSpec((None,) * len(lead) + (slab, cols), lambda i, lead=lead: lead + (i, 0)))
        args.append(stack)
        out_specs.append(pl.BlockSpec((slab, cols), lambda i: (i, 0)))
        out_shape.append(jax.ShapeDtypeStruct((rows, cols), BF16))
    n_cast = len(out_specs) - 1
    if cond is not None:
        e_col, ada_w, ada_b, layer = cond
        n = ada_w.shape[-1]
        slabs = n // COND_COLS
        assert n % COND_COLS == 0 and slabs <= steps
        col = lambda i: jnp.minimum(i, slabs - 1)
        in_specs += [_resident((d, 1)),
                     pl.BlockSpec((None, d, COND_COLS), lambda i: (layer, 0, col(i))),
                     pl.BlockSpec((None, 1, COND_COLS), lambda i: (layer, 0, col(i)))]
        args += [e_col, ada_w, ada_b]
        out_specs.append(pl.BlockSpec((1, COND_COLS), lambda i: (0, col(i))))
        out_shape.append(jax.ShapeDtypeStruct((1, n), F32))
    out = pl.pallas_call(
        functools.partial(_ffn_kernel, n_cast=n_cast, has_proj=proj is not None,
                          has_cond=0 if cond is None else slabs),
        grid=(steps,),
        in_specs=in_specs,
        out_specs=out_specs,
        out_shape=out_shape,
        scratch_shapes=scratch,
        compiler_params=_params(("arbitrary",)),
        name="ffn",
    )(*args)
    return out[0], tuple(out[1:1 + n_cast]), (out[1 + n_cast] if cond is not None else None)


def _conv_stages(x_ref, g_ref, mod_ref, win_ref, bin_ref, dw_ref, dwb_ref, lng_ref, lnb_ref,
                 wout_ref, bout_ref, o_ref, u_ref, us_ref, v_ref):
    d = x_ref.shape[1]
    gate = 1.0 + mod_ref[2:3, :]
    base = CONV_HALO - (CONV_WIDTH - 1)

    def glu_in(k):
        rs = slice(k * CONV_SUB, (k + 1) * CONV_SUB)
        h = _modulate(x_ref[rs, :], g_ref[...], mod_ref).astype(BF16)
        uv = jnp.dot(h, win_ref[...], preferred_element_type=F32) + bin_ref[...]
        u_ref[CONV_HALO + k * CONV_SUB:CONV_HALO + (k + 1) * CONV_SUB, :] = uv[:, :d] * _sigmoid(uv[:, d:])

    def shifted(k):
        lo = k * CONV_SUB
        for b in range(1, SUBLANES):
            us_ref[b - 1] = u_ref[lo + b:lo + b + us_ref.shape[1], :]

    def depthwise(k, c):
        lo = k * CONV_SUB
        cs = slice(c * LANES, (c + 1) * LANES)
        taps = [jnp.broadcast_to(dw_ref[j:j + 1, cs], (SUBLANES, LANES)) for j in range(CONV_WIDTH)]
        bias = jnp.broadcast_to(dwb_ref[:, cs], (SUBLANES, LANES))
        for r in range(0, CONV_SUB, SUBLANES):
            acc = bias
            for j in range(CONV_WIDTH):
                a, b = divmod(base + j, SUBLANES)
                r0 = r + a * SUBLANES
                src = u_ref[lo + r0:lo + r0 + SUBLANES, cs] if b == 0 else us_ref[b - 1, r0:r0 + SUBLANES, cs]
                acc = acc + taps[j] * src
            v_ref[r:r + SUBLANES, cs] = acc

    def norm_out(k):
        rs = slice(k * CONV_SUB, (k + 1) * CONV_SUB)
        v = v_ref[...]
        mu = jnp.mean(v, axis=-1, keepdims=True)
        vc = v - mu
        var = jnp.mean(vc * vc, axis=-1, keepdims=True)
        y = vc * lax.rsqrt(var + EPS) * lng_ref[...] + lnb_ref[...]
        y = (y * _sigmoid(y)).astype(BF16)
        y = jnp.dot(y, wout_ref[...], preferred_element_type=F32) + bout_ref[...]
        o_ref[rs, :] = x_ref[rs, :] + gate * y

    return glu_in, shifted, depthwise, norm_out


def _conv_kernel(x_ref, g_ref, mod_ref, win_ref, bin_ref, dw_ref, dwb_ref, lng_ref, lnb_ref,
                 wout_ref, bout_ref, o_ref, u_ref, us_ref, v_ref):
    tm, d = x_ref.shape
    i = pl.program_id(0)

    @pl.when(i == 0)
    def _():
        u_ref[0:CONV_HALO, :] = jnp.zeros((CONV_HALO, d), F32)

    @pl.when(i > 0)
    def _():
        u_ref[0:CONV_HALO, :] = u_ref[tm:tm + CONV_HALO, :]

    glu_in, shifted, depthwise, norm_out = _conv_stages(
        x_ref, g_ref, mod_ref, win_ref, bin_ref, dw_ref, dwb_ref, lng_ref, lnb_ref, wout_ref, bout_ref,
        o_ref, u_ref, us_ref, v_ref)

    n_sub = tm // CONV_SUB
    glu_in(0)
    for k in range(n_sub):
        if k + 1 < n_sub:
            glu_in(k + 1)
        shifted(k)
        for c in range(d // LANES):
            depthwise(k, c)
        norm_out(k)


def _conv_mixer(x, g, mod, w_in, b_in, dw, dw_b, ln_g, ln_b, w_out, b_out):
    s, d = x.shape
    assert s % ROW_TILE == 0 and ROW_TILE % CONV_SUB == 0
    row = lambda v: v.reshape(1, -1)
    return pl.pallas_call(
        _conv_kernel,
        grid=(s // ROW_TILE,),
        in_specs=[
            _row_spec(ROW_TILE, d),
            _resident((1, d)),
            _resident((3, d)),
            _resident((d, 2 * d)),
            _resident((1, 2 * d)),
            _resident((CONV_WIDTH, d)),
            _resident((1, d)),
            _resident((1, d)),
            _resident((1, d)),
            _resident((d, d)),
            _resident((1, d)),
        ],
        out_specs=_row_spec(ROW_TILE, d),
        out_shape=jax.ShapeDtypeStruct((s, d), F32),
        scratch_shapes=[
            pltpu.VMEM((CONV_HALO + ROW_TILE, d), F32),
            pltpu.VMEM((SUBLANES - 1, CONV_SUB + CONV_HALO - SUBLANES, d), F32),
            pltpu.VMEM((CONV_SUB, d), F32),
        ],
        compiler_params=_params(("arbitrary",)),
        name="conv_mixer",
    )(x, row(g), mod, w_in.astype(BF16), row(b_in), dw, row(dw_b), row(ln_g), row(ln_b),
      w_out.astype(BF16), row(b_out))


def _qkv_kernel(x_ref, g_ref, mod_ref, w_ref, qg_ref, kg_ref, o_ref):
    tm, d = x_ref.shape
    h = _modulate(x_ref[...], g_ref[...], mod_ref).astype(BF16)
    qkv = jnp.dot(h, w_ref[...], preferred_element_type=F32)
    first = lax.broadcasted_iota(jnp.int32, (1, LANES), 1) < HEAD_DIM

    def head_norm(blk, gain):
        sq = blk * blk
        ms_a = jnp.sum(jnp.where(first, sq, 0.0), axis=-1, keepdims=True) * (1.0 / HEAD_DIM)
        ms_b = jnp.sum(jnp.where(first, 0.0, sq), axis=-1, keepdims=True) * (1.0 / HEAD_DIM)
        r = jnp.where(first, lax.rsqrt(ms_a + EPS), lax.rsqrt(ms_b + EPS))
        return blk * r * gain

    nb = d // LANES
    for j in range(nb):
        cs = slice(j * LANES, (j + 1) * LANES)
        o_ref[:, cs] = (head_norm(qkv[:, cs], qg_ref[...]) * (HEAD_DIM ** -0.5 * LOG2_E)).astype(BF16)
    for j in range(nb, 2 * nb):
        cs = slice(j * LANES, (j + 1) * LANES)
        o_ref[:, cs] = head_norm(qkv[:, cs], kg_ref[...]).astype(BF16)
    o_ref[:, 2 * d:] = qkv[:, 2 * d:].astype(BF16)


def _qkv_proj(x, g, mod, w_qkv, q_g, k_g):
    s, d = x.shape
    tile_gain = lambda v: jnp.tile(v, HEADS_PER_BLOCK).reshape(1, LANES)
    return pl.pallas_call(
        _qkv_kernel,
        grid=(s // ROW_TILE,),
        in_specs=[
            _row_spec(ROW_TILE, d),
            _resident((1, d)),
            _resident((3, d)),
            _resident((d, 3 * d)),
            _resident((1, LANES)),
            _resident((1, LANES)),
        ],
        out_specs=_row_spec(ROW_TILE, 3 * d),
        out_shape=jax.ShapeDtypeStruct((s, 3 * d), BF16),
        compiler_params=_params(("arbitrary",)),
        name="qkv_proj",
    )(x, g.reshape(1, d), mod, w_qkv.astype(BF16), tile_gain(q_g), tile_gain(k_g))


def _attn_kernel(q_ref, k_ref, v_ref, o_ref, c_ref, acc_ref):
    n = ATT_TILE
    s = q_ref.shape[0]
    first = lax.broadcasted_iota(jnp.int32, (1, LANES), 1) < HEAD_DIM
    row = lax.broadcasted_iota(jnp.int32, (n, n), 0)
    col = lax.broadcasted_iota(jnp.int32, (n, n), 1)
    later_and_total = jnp.concatenate([(row > col).astype(BF16), jnp.ones((n, n), BF16)], axis=1)
    strictly_causal = jnp.concatenate([col < row, col < row], axis=1)
    zero = jnp.zeros((), BF16)

    def key_blocks(jobs, c, acc):
        k2, v2 = {}, {}
        for _, _, kb, _ in jobs:
            if id(kb) not in k2:
                k0 = pl.multiple_of(kb * n, n)
                k = k_ref[pl.ds(k0, n), :]
                v = v_ref[pl.ds(k0, n), :]
                k2[id(kb)] = jnp.concatenate([jnp.where(first, k, zero), jnp.where(first, zero, k)], axis=0)
                v2[id(kb)] = jnp.concatenate([jnp.where(first, v, zero), jnp.where(first, zero, v)], axis=0)
        z = [lax.dot_general(q, k2[id(kb)], (((1,), (1,)), ((), ())), preferred_element_type=F32)
             for _, q, kb, _ in jobs]
        sp = [jnp.maximum(zi, 0.0) + jnp.log2(1.0 + jnp.exp2(-jnp.abs(zi))) for zi in z]
        spb = [(jnp.where(strictly_causal, si, 0.0) if job[3] else si).astype(BF16) for si, job in zip(sp, jobs)]
        run_a = [jnp.dot(si[:, :n], later_and_total, preferred_element_type=F32) for si in spb]
        run_b = [jnp.dot(si[:, n:], later_and_total, preferred_element_type=F32) for si in spb]
        a = []
        for i, (u, _, _, diagonal) in enumerate(jobs):
            suffix = jnp.concatenate([run_a[i][:, :n], run_b[i][:, :n]], axis=1) + c[u]
            c[u] = c[u] + jnp.concatenate([run_a[i][:, n:], run_b[i][:, n:]], axis=1)
            ai = jnp.exp2(z[i] - sp[i] - suffix)
            if diagonal:
                ai = jnp.where(strictly_causal, ai, 0.0)
            a.append(ai.astype(BF16))
        for i, (u, _, kb, _) in enumerate(jobs):
            acc[u] = acc[u] + jnp.dot(a[i], v2[id(kb)], preferred_element_type=F32)

    def key_block(q, kb, diagonal, c, acc):
        c, acc = [c], [acc]
        key_blocks([(0, q, kb, diagonal)], c, acc)
        return c[0], acc[0]

    def query_tiles(base, n_tiles, n_fast):
        blocks = {off: base + off for off in range(-n_fast, n_tiles)}
        jobs = []
        for u in range(n_tiles):
            q = q_ref[pl.ds(pl.multiple_of(blocks[u] * n, n), n), :]
            jobs += [(u, q, blocks[u - j], j == 0) for j in range(n_fast + 1)]
        c = [jnp.zeros((n, 2 * n), F32)] * n_tiles
        acc = [jnp.zeros((n, LANES), F32)] * n_tiles
        key_blocks(jobs, c, acc)
        c_all = None
        for u in range(n_tiles):
            c_ref[u] = c[u]
            acc_ref[u] = acc[u]
            c_all = c[u] if c_all is None else jnp.minimum(c_all, c[u])

        @pl.when(jnp.min(c_all) < ATT_ZERO_WEIGHT)
        def _():
            for u in range(n_tiles):
                qb = base + u
                q = q_ref[pl.ds(pl.multiple_of(qb * n, n), n), :]

                def more(carry):
                    kb, c_min = carry
                    return jnp.logical_and(kb >= 0, c_min < ATT_ZERO_WEIGHT)

                def step(carry, u=u, q=q):
                    kb, _ = carry
                    c, acc = key_block(q, kb, False, c_ref[u], acc_ref[u])
                    c_ref[u] = c
                    acc_ref[u] = acc
                    return kb - 1, jnp.min(c)

                lax.while_loop(more, step, (qb - n_fast - 1, jnp.min(c_ref[u])))

        for u in range(n_tiles):
            q0 = pl.multiple_of((base + u) * n, n)
            o_ref[pl.ds(q0, n), :] = acc_ref[u].astype(o_ref.dtype)

    def head_tile(qb, carry):
        query_tiles(qb, 1, 0)
        return carry

    def body_tiles(g, carry):
        query_tiles(n_head + g * ATT_UNROLL, ATT_UNROLL, ATT_FAST_BLOCKS)
        return carry

    n_total = s // n
    n_head = ATT_FAST_BLOCKS + (n_total - ATT_FAST_BLOCKS) % ATT_UNROLL
    lax.fori_loop(0, n_head, head_tile, 0)
    lax.fori_loop(0, (n_total - n_head) // ATT_UNROLL, body_tiles, 0)


def _attention(qkv, d):
    s = qkv.shape[0]
    nb = d // LANES
    return pl.pallas_call(
        _attn_kernel,
        grid=(nb,),
        in_specs=[
            pl.BlockSpec((s, LANES), lambda hb: (0, hb)),
            pl.BlockSpec((s, LANES), lambda hb: (0, nb + hb)),
            pl.BlockSpec((s, LANES), lambda hb: (0, 2 * nb + hb)),
        ],
        out_specs=pl.BlockSpec((s, LANES), lambda hb: (0, hb)),
        out_shape=jax.ShapeDtypeStruct((s, d), BF16),
        scratch_shapes=[pltpu.VMEM((ATT_UNROLL, ATT_TILE, 2 * ATT_TILE), F32),
                        pltpu.VMEM((ATT_UNROLL, ATT_TILE, LANES), F32)],
        compiler_params=_params(("arbitrary",)),
        name="stickbreak_attn",
    )(qkv, qkv, qkv)


def _pool_kernel(x_ref, g_ref, mod_ref, pw_ref, pb_ref, ps_ref, o_ref, h_ref, lvl_ref):
    tm, d = x_ref.shape
    n_g = len(POOL_WINDOWS)
    gd = d // n_g
    i = pl.program_id(0)

    @pl.when(i == 0)
    def _():
        h_ref[0:POOL_HALO, :] = jnp.zeros((POOL_HALO, d), F32)

    @pl.when(i > 0)
    def _():
        h_ref[0:POOL_HALO, :] = h_ref[tm:tm + POOL_HALO, :]

    x = x_ref[...]
    h = _modulate(x, g_ref[...], mod_ref)
    h_ref[POOL_HALO:POOL_HALO + tm, :] = h
    t_pos = i * tm + lax.broadcasted_iota(jnp.int32, (tm, 1), 0)
    gate = 1.0 + mod_ref[2:3, :]

    rows = POOL_HALO + tm
    for m in range(n_g):
        back = 2 ** m
        first = SUBLANES * (m + 1)
        lanes = slice(m * gd, d)
        prev_ref = h_ref if m == 0 else lvl_ref.at[m - 1]
        level = prev_ref[first:rows, lanes] + prev_ref[first - back:rows - back, lanes]
        if m + 1 < n_g:
            lvl_ref[m, first:rows, lanes] = level
        cs = slice(m * gd, (m + 1) * gd)
        win = level[POOL_HALO - first:, 0:gd]
        cnt = jnp.minimum(t_pos + 1, POOL_WINDOWS[m]).astype(F32)
        diff = win / cnt - h[:, cs]
        y = jnp.dot(diff.astype(BF16), pw_ref[m], preferred_element_type=F32) + pb_ref[m:m + 1, :]
        o_ref[:, cs] = x[:, cs] + gate[:, cs] * (y * ps_ref[:, cs])


def _pool_mixer(x, g, mod, p_w, p_b, p_scale):
    s, d = x.shape
    ng, gd, _ = p_w.shape
    assert POOL_WINDOWS == tuple(2 ** (m + 1) for m in range(ng)) and POOL_HALO >= SUBLANES * ng
    return pl.pallas_call(
        _pool_kernel,
        grid=(s // ROW_TILE,),
        in_specs=[
            _row_spec(ROW_TILE, d),
            _resident((1, d)),
            _resident((3, d)),
            _resident((ng, gd, gd)),
            _resident((ng, gd)),
            _resident((1, d)),
        ],
        out_specs=_row_spec(ROW_TILE, d),
        out_shape=jax.ShapeDtypeStruct((s, d), F32),
        scratch_shapes=[pltpu.VMEM((POOL_HALO + ROW_TILE, d), F32),
                        pltpu.VMEM((ng - 1, POOL_HALO + ROW_TILE, d), F32)],
        compiler_params=_params(("arbitrary",)),
        name="pool_mixer",
    )(x, g.reshape(1, d), mod, p_w.astype(BF16), p_b, p_scale.reshape(1, d))


def kernel(x, c, cond_w, cond_b, ada_w, ada_b, norm_g, ffn_w1, ffn_w3, ffn_w2, a_w_in, a_b_in, a_dw, a_dw_b, a_ln_g, a_ln_b, a_w_out, a_b_out, b_w_qkv, b_q_g, b_k_g, b_w_o, p_w, p_b, p_scale):
    batch, s, d = x.shape
    assert batch == 1 and d == N_HEADS * HEAD_DIM
    assert s % ROW_TILE == 0 and s % ATT_TILE == 0 and s // ATT_TILE >= ATT_FAST_BLOCKS + ATT_UNROLL
    depth = ada_w.shape[0]
    n_mod = ada_w.shape[-1]
    e_col = _cond_embed(c, cond_w, cond_b)
    ada_b3 = ada_b.reshape(depth, 1, n_mod)
    as_mod = lambda row: row.reshape(3, 3, d)
    mod = as_mod(_cond_mod(e_col, ada_w, ada_b3, 0))
    xs = x.reshape(s, d)
    stacks = (ffn_w1, ffn_w3, ffn_w2)
    weights = tuple(w[0, 0].astype(BF16) for w in stacks)
    ia = ib = ic = 0
    for i in range(depth):
        cond = (e_col, ada_w, ada_b3, i + 1) if i + 1 < depth else None
        xs, weights, next_mod = _ffn(xs, norm_g[i, 0], mod[0], weights, [(w, (i, 1)) for w in stacks], cond=cond)
        kind = i % 3
        proj = None
        if kind == 0:
            xs = _conv_mixer(xs, norm_g[i, 1], mod[1], a_w_in[ia], a_b_in[ia], a_dw[ia], a_dw_b[ia],
                             a_ln_g[ia], a_ln_b[ia], a_w_out[ia], a_b_out[ia])
            ia += 1
        elif kind == 1:
            qkv = _qkv_proj(xs, norm_g[i, 1], mod[1], b_w_qkv[ib], b_q_g[ib], b_k_g[ib])
            proj = (_attention(qkv, d), mod[1], b_w_o[ib].astype(BF16))
            ib += 1
        else:
            xs = _pool_mixer(xs, norm_g[i, 1], mod[1], p_w[ic], p_b[ic], p_scale[ic])
            ic += 1
        following = [(w, (i + 1, 0)) for w in stacks] if i + 1 < depth else None
        xs, weights, _ = _ffn(xs, norm_g[i, 2], mod[2], weights, following, proj)
        if next_mod is not None:
            mod = as_mod(next_mod)
    return xs.reshape(batch, s, d)
```
